```python
import math
import jax, jax.numpy as jnp
from jax import lax
import numpy as np

D_MODEL = 1024
BATCH = 32
SEQ = 2048
DEPTH = 1

GDN_HEADS = 8
GDN_DK = 128
GDN_DV = 128
SSD_HEADS = 16
SSD_HEADDIM = 64
SSD_GROUPS = 2
SSD_STATE = 128
CONV_K = 4
CHUNK = 64
D_FF = 2816
FFN_CONV_K = 3
EPS = 1e-6

GDN_QK = GDN_HEADS * GDN_DK
GDN_V = GDN_HEADS * GDN_DV
SSD_D = SSD_HEADS * SSD_HEADDIM
SSD_BC = SSD_GROUPS * SSD_STATE
SSD_HPG = SSD_HEADS // SSD_GROUPS
MIX_WIDTH = GDN_V + SSD_D
GDN_CONV_CH = 2 * GDN_QK + GDN_V
SSD_CONV_CH = SSD_D + 2 * SSD_BC
IN_SPLITS = (GDN_QK, GDN_QK, GDN_V, GDN_V, GDN_HEADS, GDN_HEADS,
             SSD_D, SSD_D, SSD_BC, SSD_BC, SSD_HEADS)
D_IN_PROJ = sum(IN_SPLITS)

kernel_name = "hybrid_gdn_ssd_parallel_heads_convffn"


def rms_norm(x, w):
    xf = x.astype(jnp.float32)
    y = xf * lax.rsqrt(jnp.mean(xf * xf, axis=-1, keepdims=True) + EPS)
    return (y * w.astype(jnp.float32)).astype(x.dtype)


def l2_normalize(x):
    xf = x.astype(jnp.float32)
    return xf * lax.rsqrt(jnp.sum(xf * xf, axis=-1, keepdims=True) + EPS)


def causal_dwconv(x, w, b=None):
    k_w, ch = w.shape
    y = lax.conv_general_dilated(
        x, w[:, None, :].astype(x.dtype), window_strides=(1,), padding=[(k_w - 1, 0)],
        dimension_numbers=("NWC", "WIO", "NWC"), feature_group_count=ch)
    if b is not None:
        y = y + b.astype(x.dtype)
    return y


def gdn_chunked(q, k, v, g, beta):
    bsz, s, h, dk = q.shape
    dv = v.shape[-1]
    n = s // CHUNK

    def chunk(t):
        return jnp.moveaxis(t.reshape(bsz, n, CHUNK, h, *t.shape[3:]), 3, 1)

    q, k, v, g, beta = (chunk(t) for t in (q, k, v, g, beta))
    gc = jnp.cumsum(g, axis=-1)
    causal = jnp.tril(jnp.ones((CHUNK, CHUNK), dtype=bool))
    strict = jnp.tril(jnp.ones((CHUNK, CHUNK), dtype=bool), -1)
    decay = jnp.exp(jnp.where(causal, gc[..., :, None] - gc[..., None, :], -jnp.inf))
    kb = k * beta[..., None]
    a_in = jnp.einsum("bhnid,bhnjd->bhnij", kb, k) * decay
    lmat = jnp.where(strict, a_in, 0.0) + jnp.eye(CHUNK, dtype=a_in.dtype)
    rhs = jnp.concatenate([v * beta[..., None], kb * jnp.exp(gc)[..., None]], axis=-1)
    sol = lax.linalg.triangular_solve(lmat, rhs, left_side=True, lower=True, unit_diagonal=True)
    u, w = sol[..., :dv], sol[..., dv:]
    qk = jnp.einsum("bhnid,bhnjd->bhnij", q, k) * decay
    q_dec = q * jnp.exp(gc)[..., None]
    k_dec = k * jnp.exp(gc[..., -1:] - gc)[..., None]
    g_last = jnp.exp(gc[..., -1])

    def step(state, xs):
        u_i, w_i, qk_i, q_i, k_i, gl = xs
        v_new = u_i - jnp.einsum("bhcd,bhde->bhce", w_i, state)
        o = jnp.einsum("bhcd,bhde->bhce", q_i, state) + jnp.einsum("bhij,bhje->bhie", qk_i, v_new)
        state = state * gl[..., None, None] + jnp.einsum("bhcd,bhce->bhde", k_i, v_new)
        return state, o

    xs = tuple(jnp.moveaxis(t, 2, 0) for t in (u, w, qk, q_dec, k_dec, g_last))
    s0 = jnp.zeros((bsz, h, dk, dv), dtype=q.dtype)
    _, o = lax.scan(step, s0, xs)
    return jnp.transpose(o, (1, 0, 3, 2, 4)).reshape(bsz, s, h, dv)


def ssd_chunked(x, dt, a_neg, bmat, cmat):
    bsz, s, grp, hg, p = x.shape
    nst = bmat.shape[-1]
    n = s // CHUNK
    xdt = x * dt[..., None]
    adt = dt * a_neg

    def chunk(t):
        return jnp.moveaxis(t.reshape(bsz, n, CHUNK, *t.shape[2:]), 1, 0)

    xs = tuple(chunk(t) for t in (xdt, adt, bmat, cmat))
    causal = jnp.tril(jnp.ones((CHUNK, CHUNK), dtype=bool))

    def step(state, xs):
        xdt_i, adt_i, b_i, c_i = xs
        acs = jnp.cumsum(jnp.moveaxis(adt_i, 1, -1), axis=-1)
        lmat = jnp.exp(jnp.where(causal, acs[..., :, None] - acs[..., None, :], -jnp.inf))
        cb = jnp.einsum("blgn,bsgn->bgls", c_i, b_i)
        y_diag = jnp.einsum("bgls,bghls,bsghp->blghp", cb, lmat, xdt_i)
        y_off = jnp.einsum("blgn,bghpn,bghl->blghp", c_i, state, jnp.exp(acs))
        decay_s = jnp.exp(acs[..., -1:] - acs)
        state = state * jnp.exp(acs[..., -1])[..., None, None] + jnp.einsum(
            "bsgn,bghs,bsghp->bghpn", b_i, decay_s, xdt_i)
        return state, y_diag + y_off

    s0 = jnp.zeros((bsz, grp, hg, p, nst), dtype=x.dtype)
    _, y = lax.scan(step, s0, xs)
    return jnp.moveaxis(y, 0, 1).reshape(bsz, s, grp, hg, p)


def hybrid_layer(x, pre_mix_norm, w_in, gdn_conv_w, gdn_a_log, gdn_dt_bias, gdn_norm_w,
                 ssd_conv_w, ssd_conv_b, ssd_a_log, ssd_dt_bias, ssd_d, ssd_norm_w,
                 w_out, post_mix_norm, pre_ffn_norm, w_up, ffn_conv_w, ffn_conv_b,
                 w_down, post_ffn_norm):
    f32 = jnp.float32
    bsz, s, _ = x.shape
    h = rms_norm(x, pre_mix_norm)
    proj = h @ w_in
    offsets = np.cumsum(IN_SPLITS)[:-1].tolist()
    q, k, v, z_a, b_a, a_a, z_s, x_s, b_s, c_s, dt_s = jnp.split(proj, offsets, axis=-1)

    qkv = jax.nn.silu(causal_dwconv(jnp.concatenate([q, k, v], axis=-1), gdn_conv_w))
    q, k, v = jnp.split(qkv, [GDN_QK, 2 * GDN_QK], axis=-1)
    q = l2_normalize(q.reshape(bsz, s, GDN_HEADS, GDN_DK)) * (GDN_DK ** -0.5)
    k = l2_normalize(k.reshape(bsz, s, GDN_HEADS, GDN_DK))
    v = v.reshape(bsz, s, GDN_HEADS, GDN_DV).astype(f32)
    beta = jax.nn.sigmoid(b_a.astype(f32))
    g = -jnp.exp(gdn_a_log.astype(f32)) * jax.nn.softplus(a_a.astype(f32) + gdn_dt_bias.astype(f32))
    o_a = gdn_chunked(q, k, v, g, beta)
    o_a = rms_norm(o_a, gdn_norm_w) * jax.nn.silu(z_a.reshape(bsz, s, GDN_HEADS, GDN_DV).astype(f32))
    o_a = o_a.reshape(bsz, s, GDN_V).astype(x.dtype)

    xbc = jax.nn.silu(causal_dwconv(jnp.concatenate([x_s, b_s, c_s], axis=-1), ssd_conv_w, ssd_conv_b))
    xs_, bm, cm = jnp.split(xbc, [SSD_D, SSD_D + SSD_BC], axis=-1)
    xs_ = xs_.reshape(bsz, s, SSD_GROUPS, SSD_HPG, SSD_HEADDIM).astype(f32)
    bm = bm.reshape(bsz, s, SSD_GROUPS, SSD_STATE).astype(f32)
    cm = cm.reshape(bsz, s, SSD_GROUPS, SSD_STATE).astype(f32)
    dt = jax.nn.softplus(dt_s.astype(f32) + ssd_dt_bias.astype(f32)).reshape(bsz, s, SSD_GROUPS, SSD_HPG)
    a_neg = -jnp.exp(ssd_a_log.astype(f32)).reshape(SSD_GROUPS, SSD_HPG)
    y = ssd_chunked(xs_, dt, a_neg, bm, cm)
    y = y + ssd_d.astype(f32).reshape(SSD_GROUPS, SSD_HPG)[..., None] * xs_
    y = y.reshape(bsz, s, SSD_D) * jax.nn.silu(z_s.astype(f32))
    y = rms_norm(y.reshape(bsz, s, SSD_GROUPS, SSD_D // SSD_GROUPS),
                 ssd_norm_w.reshape(SSD_GROUPS, SSD_D // SSD_GROUPS))
    o_s = y.reshape(bsz, s, SSD_D).astype(x.dtype)

    mix = jnp.concatenate([o_a, o_s], axis=-1) @ w_out
    x = x + rms_norm(mix, post_mix_norm)

    h = rms_norm(x, pre_ffn_norm)
    u = causal_dwconv(h @ w_up, ffn_conv_w, ffn_conv_b)
    gate, up = jnp.split(u, 2, axis=-1)
    f = (jax.nn.silu(gate) * up) @ w_down
    return x + rms_norm(f, post_ffn_norm)


def setup_inputs(seed: int = 0) -> dict:
    key = jax.random.key(seed)
    ks = jax.random.split(key, 24)
    f32 = jnp.float32
    L = DEPTH

    def nrm(k, shape, scale):
        return jax.random.normal(k, shape, f32) * scale

    def gain(k, shape):
        return 1.0 + 0.05 * jax.random.normal(k, shape, f32)

    def dt_bias(k, shape):
        dt = jnp.exp(jax.random.uniform(k, shape, f32, math.log(1e-3), math.log(1e-1)))
        return dt + jnp.log(-jnp.expm1(-dt))

    def a_log(k, shape):
        return jnp.log(jax.random.uniform(k, shape, f32, 1.0, 16.0))

    return {
        "x": nrm(ks[0], (BATCH, SEQ, D_MODEL), 1.0),
        "pre_mix_norm": gain(ks[1], (L, D_MODEL)),
        "w_in": nrm(ks[2], (L, D_MODEL, D_IN_PROJ), D_MODEL ** -0.5),
        "gdn_conv_w": nrm(ks[3], (L, CONV_K, GDN_CONV_CH), CONV_K ** -0.5),
        "gdn_a_log": a_log(ks[4], (L, GDN_HEADS)),
        "gdn_dt_bias": dt_bias(ks[5], (L, GDN_HEADS)),
        "gdn_norm_w": gain(ks[6], (L, GDN_DV)),
        "ssd_conv_w": nrm(ks[7], (L, CONV_K, SSD_CONV_CH), CONV_K ** -0.5),
        "ssd_conv_b": nrm(ks[8], (L, SSD_CONV_CH), 0.02),
        "ssd_a_log": a_log(ks[9], (L, SSD_HEADS)),
        "ssd_dt_bias": dt_bias(ks[10], (L, SSD_HEADS)),
        "ssd_d": gain(ks[11], (L, SSD_HEADS)),
        "ssd_norm_w": gain(ks[12], (L, SSD_D)),
        "w_out": nrm(ks[13], (L, MIX_WIDTH, D_MODEL), MIX_WIDTH ** -0.5),
        "post_mix_norm": gain(ks[14], (L, D_MODEL)),
        "pre_ffn_norm": gain(ks[15], (L, D_MODEL)),
        "w_up": nrm(ks[16], (L, D_MODEL, 2 * D_FF), D_MODEL ** -0.5),
        "ffn_conv_w": nrm(ks[17], (L, FFN_CONV_K, 2 * D_FF), FFN_CONV_K ** -0.5),
        "ffn_conv_b": nrm(ks[18], (L, 2 * D_FF), 0.02),
        "w_down": nrm(ks[19], (L, D_FF, D_MODEL), D_FF ** -0.5),
        "post_ffn_norm": gain(ks[20], (L, D_MODEL)),
    }


def reference(x, pre_mix_norm, w_in, gdn_conv_w, gdn_a_log, gdn_dt_bias, gdn_norm_w,
              ssd_conv_w, ssd_conv_b, ssd_a_log, ssd_dt_bias, ssd_d, ssd_norm_w,
              w_out, post_mix_norm, pre_ffn_norm, w_up, ffn_conv_w, ffn_conv_b,
              w_down, post_ffn_norm):
    for l in range(DEPTH):
        x = hybrid_layer(x, pre_mix_norm[l], w_in[l], gdn_conv_w[l], gdn_a_log[l], gdn_dt_bias[l],
                         gdn_norm_w[l], ssd_conv_w[l], ssd_conv_b[l], ssd_a_log[l], ssd_dt_bias[l],
                         ssd_d[l], ssd_norm_w[l], w_out[l], post_mix_norm[l], pre_ffn_norm[l],
                         w_up[l], ffn_conv_w[l], ffn_conv_b[l], w_down[l], post_ffn_norm[l])
    return x
```

```python
import functools

import jax
import jax.numpy as jnp
from jax import lax
from jax.experimental import pallas as pl
from jax.experimental.pallas import tpu as pltpu

F32 = jnp.float32
BF16 = jnp.bfloat16

D_MODEL = 1024
GDN_HEADS = 8
GDN_DK = 128
GDN_DV = 128
SSD_HEADS = 16
SSD_HEADDIM = 64
SSD_GROUPS = 2
SSD_STATE = 128
CONV_K = 4
CHUNK = 64
D_FF = 2816
FFN_CONV_K = 3
EPS = 1e-6

GDN_QK = GDN_HEADS * GDN_DK
GDN_V = GDN_HEADS * GDN_DV
SSD_D = SSD_HEADS * SSD_HEADDIM
SSD_BC = SSD_GROUPS * SSD_STATE
MIX_WIDTH = GDN_V + SSD_D
GDN_CONV_CH = 2 * GDN_QK + GDN_V
SSD_CONV_CH = SSD_D + 2 * SSD_BC
CONV_CH = GDN_CONV_CH + SSD_CONV_CH
IN_SPLITS = (GDN_QK, GDN_QK, GDN_V, GDN_V, GDN_HEADS, GDN_HEADS, SSD_D, SSD_D, SSD_BC, SSD_BC, SSD_HEADS)

LANES = 128
SUBLANES = 8
VMEM_LIMIT_BYTES = 56 * 1024 * 1024

COL_XS = GDN_CONV_CH
COL_B = COL_XS + SSD_D
COL_C = COL_B + SSD_BC
COL_ZA = CONV_CH
COL_ZS = COL_ZA + GDN_V
COL_SMALL = COL_ZS + SSD_D
W_IN_COLS = COL_SMALL + LANES
LANE_G = 0
LANE_DT = GDN_HEADS
LANE_BETA = GDN_HEADS + SSD_HEADS
N_GATE_LANES = GDN_HEADS + SSD_HEADS

MIX_TILE = 256
FFN_TILE = 512
MIX_CW = 512
FFN_CW = 256
CARRY = SUBLANES
BLK = 16


def _bf(x):
    return x.astype(BF16)


def _dot(a, b):
    return jnp.dot(_bf(a), _bf(b), preferred_element_type=F32)


def _dot_nt(a, b):
    return lax.dot_general(_bf(a), _bf(b), (((1,), (1,)), ((), ())), preferred_element_type=F32)


def _rmsnorm(x, w):
    return x * lax.rsqrt(jnp.mean(x * x, axis=-1, keepdims=True) + EPS) * w


def _sigmoid(x):
    return 1.0 / (1.0 + jnp.exp(-x))


def _silu(x):
    return x * _sigmoid(x)


def _softplus(x):
    return jnp.maximum(x, 0.0) + jnp.log1p(jnp.exp(-jnp.abs(x)))


def _chunk_cumsum(x):
    row = lax.broadcasted_iota(jnp.int32, x.shape, 0)
    s = 1
    while s < CHUNK:
        x = x + jnp.where(row >= s, pltpu.roll(x, s, axis=0), 0.0)
        s *= 2
    return x


def _unit_lower_inverse(a, eye, blk_mask):
    d = jnp.where(blk_mask, a, 0.0)
    n = a - d
    m = eye - d
    p = _dot(d, d)
    m = m + _dot(m, p)
    p = _dot(p, p)
    m = m + _dot(m, p)
    p = _dot(p, p)
    m = m + _dot(m, p)
    e = _dot(m, n)
    f = eye - e
    f = f + _dot(f, _dot(e, e))
    return _dot(f, m)


def _mixer_kernel(x_ref, nw_ref, win_ref, cw_ref, cb_ref, sp_ref, gnw_ref, dsk_ref, snw_ref, wout_ref, pnw_ref,
                  o_ref, h_s, pre_s, act_s, col_s, oy_s, z_s, cat_s, gst_s, sst_s):
    tile = x_ref.shape[1]
    n_chunks = tile // CHUNK
    j = pl.program_id(1)

    @pl.when(j == 0)
    def _():
        pre_s[0:CARRY, :] = jnp.zeros((CARRY, CONV_CH), F32)
        gst_s[...] = jnp.zeros(gst_s.shape, F32)
        sst_s[...] = jnp.zeros(sst_s.shape, F32)

    x = x_ref[0]
    hb = _bf(_rmsnorm(x, nw_ref[...]))
    h_s[...] = hb
    for cbk in range(CONV_CH // MIX_CW):
        cols = slice(cbk * MIX_CW, (cbk + 1) * MIX_CW)
        pre_s[CARRY:CARRY + tile, cols] = jnp.dot(hb, win_ref[:, cols], preferred_element_type=F32)

    for cbk in range(CONV_CH // MIX_CW):
        cols = slice(cbk * MIX_CW, (cbk + 1) * MIX_CW)
        acc = cb_ref[:, cols] + cw_ref[0:1, cols] * pre_s[CARRY - CONV_K + 1:CARRY - CONV_K + 1 + tile, cols]
        for k in range(1, CONV_K):
            r0 = CARRY - CONV_K + 1 + k
            acc = acc + cw_ref[k:k + 1, cols] * pre_s[r0:r0 + tile, cols]
        act_s[:, cols] = _silu(acc)
        pre_s[0:CARRY, cols] = pre_s[tile:tile + CARRY, cols]

    for hd in range(GDN_HEADS):
        for base, scale in ((0, GDN_DK ** -0.5), (GDN_QK, 1.0)):
            cols = slice(base + hd * GDN_DK, base + (hd + 1) * GDN_DK)
            v = act_s[:, cols]
            act_s[:, cols] = v * (lax.rsqrt(jnp.sum(v * v, axis=-1, keepdims=True) + EPS) * scale)

    small = jnp.dot(hb, win_ref[:, COL_SMALL:COL_SMALL + LANES], preferred_element_type=F32)
    lane1 = lax.broadcasted_iota(jnp.int32, (1, LANES), 1)
    a_row = jnp.where(lane1 < N_GATE_LANES, -jnp.exp(sp_ref[1:2, :]), 0.0)
    sp = _softplus(small + sp_ref[0:1, :])
    col_s[0] = sp * a_row
    col_s[1] = sp
    col_s[2] = pltpu.roll(_sigmoid(small), LANES - LANE_BETA, axis=1)

    z_s[:, 0:GDN_V] = jnp.dot(hb, win_ref[:, COL_ZA:COL_ZA + GDN_V], preferred_element_type=F32)
    z_s[:, GDN_V:MIX_WIDTH] = jnp.dot(hb, win_ref[:, COL_ZS:COL_ZS + SSD_D], preferred_element_type=F32)

    r_cc = lax.broadcasted_iota(jnp.int32, (CHUNK, CHUNK), 0)
    c_cc = lax.broadcasted_iota(jnp.int32, (CHUNK, CHUNK), 1)
    tril = r_cc >= c_cc
    strict = r_cc > c_cc
    blk_shift = BLK.bit_length() - 1
    blk_mask = (r_cc >> blk_shift) == (c_cc >> blk_shift)
    eye = (r_cc == c_cc).astype(F32)
    lane_cl = lax.broadcasted_iota(jnp.int32, (CHUNK, LANES), 1)
    lane_sl = lax.broadcasted_iota(jnp.int32, (SSD_STATE, LANES), 1)
    neg_inf = -jnp.inf

    def chunk_body(c, carry):
        r0 = pl.multiple_of(c * CHUNK, CHUNK)
        rows = pl.ds(r0, CHUNK)
        gad = col_s[0, rows, :]
        dtv = col_s[1, rows, :]
        beta0 = col_s[2, rows, :]
        cs = _chunk_cumsum(gad)
        last = cs[CHUNK - 1:CHUNK, :]
        ecs = jnp.exp(cs)
        edl = jnp.exp(last - cs)
        gl = jnp.exp(last)
        is_g = lane_cl < GDN_HEADS
        cs_t = cs.T
        bd_t = jnp.where(is_g, beta0, dtv).T
        be_t = jnp.where(is_g, beta0 * ecs, edl * dtv).T
        edl_t = edl.T

        for hd in range(GDN_HEADS):
            cols = slice(hd * GDN_DK, (hd + 1) * GDN_DK)
            q = act_s[rows, cols]
            k = act_s[rows, GDN_QK + hd * GDN_DK:GDN_QK + (hd + 1) * GDN_DK]
            v = act_s[rows, 2 * GDN_QK + hd * GDN_DV:2 * GDN_QK + (hd + 1) * GDN_DV]
            ln = LANE_G + hd
            dec = jnp.exp(jnp.where(tril, cs[:, ln:ln + 1] - cs_t[ln:ln + 1, :], neg_inf))
            kk = _dot_nt(k, k)
            a = jnp.where(strict, kk * beta0[:, ln:ln + 1] * dec, 0.0)
            tinv = _unit_lower_inverse(a, eye, blk_mask)
            u = _dot(tinv * bd_t[ln:ln + 1, :], v)
            w = _dot(tinv * be_t[ln:ln + 1, :], k)
            qk = _dot_nt(q, k) * dec
            st = gst_s[hd]
            v_new = u - _dot(w, st)
            o = ecs[:, ln:ln + 1] * _dot(q, st) + _dot(qk, v_new)
            gst_s[hd] = st * gl[:, ln:ln + 1] + _dot(k.T * edl_t[ln:ln + 1, :], v_new)
            oy_s[rows, cols] = o

        cb = []
        b_t = []
        c_g = []
        for g in range(SSD_GROUPS):
            bm = act_s[rows, COL_B + g * SSD_STATE:COL_B + (g + 1) * SSD_STATE]
            cm = act_s[rows, COL_C + g * SSD_STATE:COL_C + (g + 1) * SSD_STATE]
            cb.append(_dot_nt(cm, bm))
            b_t.append(bm.T)
            c_g.append(cm)
        pairs_per_group = SSD_HEADS // SSD_GROUPS // 2
        for p in range(SSD_HEADS // 2):
            g = p // pairs_per_group
            xp = act_s[rows, COL_XS + p * LANES:COL_XS + (p + 1) * LANES]
            st = sst_s[p]
            ys = []
            sn = []
            for hh in range(2):
                ln = LANE_DT + 2 * p + hh
                lmat = jnp.exp(jnp.where(tril, cs[:, ln:ln + 1] - cs_t[ln:ln + 1, :], neg_inf))
                mh = cb[g] * lmat * bd_t[ln:ln + 1, :]
                ce = c_g[g] * ecs[:, ln:ln + 1]
                ys.append(_dot(mh, xp) + _dot(ce, st))
                sn.append(st * gl[:, ln:ln + 1] + _dot(b_t[g] * be_t[ln:ln + 1, :], xp))
            oy_s[rows, GDN_V + p * LANES:GDN_V + (p + 1) * LANES] = jnp.where(lane_cl < SSD_HEADDIM, ys[0], ys[1])
            sst_s[p] = jnp.where(lane_sl < SSD_HEADDIM, sn[0], sn[1])
        return carry

    lax.fori_loop(0, n_chunks, chunk_body, 0)

    for hd in range(GDN_HEADS):
        cols = slice(hd * GDN_DV, (hd + 1) * GDN_DV)
        o = oy_s[:, cols]
        o = o * lax.rsqrt(jnp.mean(o * o, axis=-1, keepdims=True) + EPS) * gnw_ref[:, cols]
        cat_s[:, cols] = _bf(o * _silu(z_s[:, cols]))
    gw = SSD_D // SSD_GROUPS
    for g in range(SSD_GROUPS):
        cols = slice(GDN_V + g * gw, GDN_V + (g + 1) * gw)
        xs = act_s[:, COL_XS + g * gw:COL_XS + (g + 1) * gw]
        y = oy_s[:, cols] + dsk_ref[:, g * gw:(g + 1) * gw] * xs
        y = y * _silu(z_s[:, cols])
        y = y * lax.rsqrt(jnp.mean(y * y, axis=-1, keepdims=True) + EPS) * snw_ref[:, g * gw:(g + 1) * gw]
        cat_s[:, cols] = _bf(y)

    mix = jnp.dot(cat_s[...], wout_ref[...], preferred_element_type=F32)
    o_ref[0] = x + _rmsnorm(mix, pnw_ref[...])


def _ffn_kernel(x_ref, nw_ref, wup_ref, cw_ref, cb_ref, wdn_ref, pnw_ref, o_ref, ub_s, carry_s, act_s):
    tile = x_ref.shape[1]
    j = pl.program_id(1)

    @pl.when(j == 0)
    def _():
        carry_s[...] = jnp.zeros(carry_s.shape, F32)

    x = x_ref[0]
    hb = _bf(_rmsnorm(x, nw_ref[...]))
    first = CARRY - FFN_CONV_K + 1
    for c in range(D_FF // FFN_CW):
        slot = c % 2
        halves = []
        for base in (0, D_FF):
            cols = slice(base + c * FFN_CW, base + (c + 1) * FFN_CW)
            sc = slice(0, FFN_CW) if base == 0 else slice(FFN_CW, 2 * FFN_CW)
            ub_s[slot, 0:CARRY, sc] = carry_s[:, cols]
            ub_s[slot, CARRY:CARRY + tile, sc] = jnp.dot(hb, wup_ref[:, cols], preferred_element_type=F32)
            acc = cb_ref[:, cols] + cw_ref[0:1, cols] * ub_s[slot, first:first + tile, sc]
            for k in range(1, FFN_CONV_K):
                acc = acc + cw_ref[k:k + 1, cols] * ub_s[slot, first + k:first + k + tile, sc]
            carry_s[:, cols] = ub_s[slot, tile:tile + CARRY, sc]
            halves.append(acc)
        act_s[:, c * FFN_CW:(c + 1) * FFN_CW] = _bf(_silu(halves[0]) * halves[1])
    f = jnp.dot(act_s[...], wdn_ref[...], preferred_element_type=F32)
    o_ref[0] = x + _rmsnorm(f, pnw_ref[...])


def _const_spec(shape):
    return pl.BlockSpec(shape, lambda b, j: (0,) * len(shape), pipeline_mode=pl.Buffered(1))


def _mixer_call(x, nw, win, cw, cb, sp, gnw, dsk, snw, wout, pnw):
    bsz, seq, _ = x.shape
    tile = MIX_TILE
    assert seq % tile == 0 and tile % CHUNK == 0
    tok_spec = pl.BlockSpec((1, tile, D_MODEL), lambda b, j: (b, j, 0))
    consts = (nw, win, cw, cb, sp, gnw, dsk, snw, wout, pnw)
    return pl.pallas_call(
        _mixer_kernel,
        grid=(bsz, seq // tile),
        in_specs=[tok_spec] + [_const_spec(c.shape) for c in consts],
        out_specs=tok_spec,
        out_shape=jax.ShapeDtypeStruct(x.shape, x.dtype),
        scratch_shapes=[
            pltpu.VMEM((tile, D_MODEL), BF16),
            pltpu.VMEM((CARRY + tile, CONV_CH), F32),
            pltpu.VMEM((tile, CONV_CH), F32),
            pltpu.VMEM((3, tile, LANES), F32),
            pltpu.VMEM((tile, MIX_WIDTH), F32),
            pltpu.VMEM((tile, MIX_WIDTH), F32),
            pltpu.VMEM((tile, MIX_WIDTH), BF16),
            pltpu.VMEM((GDN_HEADS, GDN_DK, GDN_DV), F32),
            pltpu.VMEM((SSD_HEADS // 2, SSD_STATE, LANES), F32),
        ],
        compiler_params=pltpu.CompilerParams(
            dimension_semantics=("arbitrary", "arbitrary"), vmem_limit_bytes=VMEM_LIMIT_BYTES),
        name="mixer",
    )(x, *consts)


def _ffn_call(x, nw, wup, cw, cb, wdn, pnw):
    bsz, seq, _ = x.shape
    tile = FFN_TILE
    assert seq % tile == 0
    tok_spec = pl.BlockSpec((1, tile, D_MODEL), lambda b, j: (b, j, 0))
    consts = (nw, wup, cw, cb, wdn, pnw)
    return pl.pallas_call(
        _ffn_kernel,
        grid=(bsz, seq // tile),
        in_specs=[tok_spec] + [_const_spec(c.shape) for c in consts],
        out_specs=tok_spec,
        out_shape=jax.ShapeDtypeStruct(x.shape, x.dtype),
        scratch_shapes=[
            pltpu.VMEM((2, CARRY + tile, 2 * FFN_CW), F32),
            pltpu.VMEM((CARRY, 2 * D_FF), F32),
            pltpu.VMEM((tile, D_FF), BF16),
        ],
        compiler_params=pltpu.CompilerParams(
            dimension_semantics=("arbitrary", "arbitrary"), vmem_limit_bytes=VMEM_LIMIT_BYTES),
        name="ffn",
    )(x, *consts)


def _row(v):
    return v.reshape(1, -1).astype(F32)


def _layer(x, pre_mix_norm, w_in, gdn_conv_w, gdn_a_log, gdn_dt_bias, gdn_norm_w, ssd_conv_w, ssd_conv_b,
           ssd_a_log, ssd_dt_bias, ssd_d, ssd_norm_w, w_out, post_mix_norm, pre_ffn_norm, w_up, ffn_conv_w,
           ffn_conv_b, w_down, post_ffn_norm):
    offs = [0]
    for n in IN_SPLITS:
        offs.append(offs[-1] + n)
    seg = [w_in[:, offs[i]:offs[i + 1]] for i in range(len(IN_SPLITS))]
    wq, wk, wv, wza, wb, wa, wzs, wxs, wbs, wcs, wdt = seg
    pad = jnp.zeros((D_MODEL, LANES - N_GATE_LANES - GDN_HEADS), w_in.dtype)
    win = _bf(jnp.concatenate([wq, wk, wv, wxs, wbs, wcs, wza, wzs, wa, wdt, wb, pad], axis=1))
    cw = jnp.concatenate([gdn_conv_w, ssd_conv_w], axis=1).astype(F32)
    cb = jnp.concatenate([jnp.zeros((GDN_CONV_CH,), F32), ssd_conv_b.astype(F32)]).reshape(1, CONV_CH)
    zpad = jnp.zeros((LANES - N_GATE_LANES,), F32)
    sp = jnp.zeros((SUBLANES, LANES), F32)
    sp = sp.at[0].set(jnp.concatenate([gdn_dt_bias.astype(F32), ssd_dt_bias.astype(F32), zpad]))
    sp = sp.at[1].set(jnp.concatenate([gdn_a_log.astype(F32), ssd_a_log.astype(F32), zpad]))
    gnw = _row(jnp.tile(gdn_norm_w, GDN_HEADS))
    dsk = _row(jnp.repeat(ssd_d, SSD_HEADDIM))
    x1 = _mixer_call(x, _row(pre_mix_norm), win, cw, cb, sp, gnw, dsk, _row(ssd_norm_w), _bf(w_out),
                     _row(post_mix_norm))
    return _ffn_call(x1, _row(pre_ffn_norm), _bf(w_up), ffn_conv_w.astype(F32), _row(ffn_conv_b), _bf(w_down),
                     _row(post_ffn_norm))


def kernel(x, pre_mix_norm, w_in, gdn_conv_w, gdn_a_log, gdn_dt_bias, gdn_norm_w, ssd_conv_w, ssd_conv_b, ssd_a_log, ssd_dt_bias, ssd_d, ssd_norm_w, w_out, post_mix_norm, pre_ffn_norm, w_up, ffn_conv_w, ffn_conv_b, w_down, post_ffn_norm):
    params = (pre_mix_norm, w_in, gdn_conv_w, gdn_a_log, gdn_dt_bias, gdn_norm_w, ssd_conv_w, ssd_conv_b, ssd_a_log,
              ssd_dt_bias, ssd_d, ssd_norm_w, w_out, post_mix_norm, pre_ffn_norm, w_up, ffn_conv_w, ffn_conv_b,
              w_down, post_ffn_norm)
    for layer in range(pre_mix_norm.shape[0]):
        x = _layer(x, *(p[layer] for p in params))
    return x
```

```python
import functools

import jax
import jax.numpy as jnp
from jax import lax
from jax.experimental import pallas as pl
from jax.experimental.pallas import tpu as pltpu

F32 = jnp.float32
BF16 = jnp.bfloat16

D_MODEL = 1024
GDN_HEADS = 8
GDN_DK = 128
GDN_DV = 128
SSD_HEADS = 16
SSD_HEADDIM = 64
SSD_GROUPS = 2
SSD_STATE = 128
CONV_K = 4
CHUNK = 64
D_FF = 2816
FFN_CONV_K = 3
EPS = 1e-6

GDN_QK = GDN_HEADS * GDN_DK
GDN_V = GDN_HEADS * GDN_DV
SSD_D = SSD_HEADS * SSD_HEADDIM
SSD_BC = SSD_GROUPS * SSD_STATE
MIX_WIDTH = GDN_V + SSD_D
GDN_CONV_CH = 2 * GDN_QK + GDN_V
SSD_CONV_CH = SSD_D + 2 * SSD_BC
CONV_CH = GDN_CONV_CH + SSD_CONV_CH
IN_SPLITS = (GDN_QK, GDN_QK, GDN_V, GDN_V, GDN_HEADS, GDN_HEADS, SSD_D, SSD_D, SSD_BC, SSD_BC, SSD_HEADS)

LANES = 128
SUBLANES = 8
VMEM_LIMIT_BYTES = 56 * 1024 * 1024

COL_XS = GDN_CONV_CH
COL_B = COL_XS + SSD_D
COL_C = COL_B + SSD_BC
COL_ZA = CONV_CH
COL_ZS = COL_ZA + GDN_V
COL_SMALL = COL_ZS + SSD_D
W_IN_COLS = COL_SMALL + LANES
LANE_G = 0
LANE_DT = GDN_HEADS
LANE_BETA = GDN_HEADS + SSD_HEADS
N_GATE_LANES = GDN_HEADS + SSD_HEADS

MIX_TILE = 256
FFN_TILE = 512
MIX_CW = 512
FFN_CW = 256
CARRY = SUBLANES
BLK = 16


def _bf(x):
    return x.astype(BF16)


def _dot(a, b):
    return jnp.dot(_bf(a), _bf(b), preferred_element_type=F32)


def _dot_nt(a, b):
    return lax.dot_general(_bf(a), _bf(b), (((1,), (1,)), ((), ())), preferred_element_type=F32)


def _rmsnorm(x, w):
    return x * lax.rsqrt(jnp.mean(x * x, axis=-1, keepdims=True) + EPS) * w


def _sigmoid(x):
    return 1.0 / (1.0 + jnp.exp(-x))


def _silu(x):
    return x * _sigmoid(x)


def _softplus(x):
    return jnp.maximum(x, 0.0) + jnp.log1p(jnp.exp(-jnp.abs(x)))


def _chunk_cumsum(x):
    row = lax.broadcasted_iota(jnp.int32, x.shape, 0)
    s = 1
    while s < CHUNK:
        x = x + jnp.where(row >= s, pltpu.roll(x, s, axis=0), 0.0)
        s *= 2
    return x


def _unit_lower_inverse(a, eye, blk_mask, fill):
    hs = range(len(a))
    d = [jnp.where(blk_mask, a[h], 0.0) for h in hs]
    n = [_bf(a[h] - d[h]) for h in hs]
    m = [eye - d[h] for h in hs]
    p = [_bf(d[h]) for h in hs]
    p = [_dot(p[h], p[h]) for h in hs]
    fill(3)
    for _ in range(2):
        m = [m[h] + _dot(m[h], p[h]) for h in hs]
        p = [_bf(p[h]) for h in hs]
        p = [_dot(p[h], p[h]) for h in hs]
        fill(3)
    m = [_bf(m[h] + _dot(m[h], p[h])) for h in hs]
    fill(3)
    e = [_dot(m[h], n[h]) for h in hs]
    fill(3)
    f = [eye - e[h] for h in hs]
    e = [_bf(e[h]) for h in hs]
    e = [_dot(e[h], e[h]) for h in hs]
    fill(3)
    f = [f[h] + _dot(f[h], e[h]) for h in hs]
    fill(3)
    t = [_dot(f[h], m[h]) for h in hs]
    fill(3)
    return t


def _mixer_kernel(x_ref, nw_ref, win_ref, cw_ref, cb_ref, sp_ref, gnw_ref, dsk_ref, snw_ref, wout_ref, pnw_ref,
                  o_ref, h_s, pre_s, act_s, col_s, oy_s, z_s, cat_s, gst_s, sst_s):
    tile = x_ref.shape[1]
    n_chunks = tile // CHUNK
    j = pl.program_id(1)

    @pl.when(j == 0)
    def _():
        pre_s[0:CARRY, :] = jnp.zeros((CARRY, CONV_CH), F32)
        gst_s[...] = jnp.zeros(gst_s.shape, F32)
        sst_s[...] = jnp.zeros(sst_s.shape, F32)

    x = x_ref[0]
    hb = _bf(_rmsnorm(x, nw_ref[...]))
    h_s[...] = hb
    for cbk in range(CONV_CH // MIX_CW):
        cols = slice(cbk * MIX_CW, (cbk + 1) * MIX_CW)
        pre_s[CARRY:CARRY + tile, cols] = jnp.dot(hb, win_ref[:, cols], preferred_element_type=F32)

    for cbk in range(CONV_CH // MIX_CW):
        cols = slice(cbk * MIX_CW, (cbk + 1) * MIX_CW)
        acc = cb_ref[:, cols] + cw_ref[0:1, cols] * pre_s[CARRY - CONV_K + 1:CARRY - CONV_K + 1 + tile, cols]
        for k in range(1, CONV_K):
            r0 = CARRY - CONV_K + 1 + k
            acc = acc + cw_ref[k:k + 1, cols] * pre_s[r0:r0 + tile, cols]
        act_s[:, cols] = _silu(acc)
        pre_s[0:CARRY, cols] = pre_s[tile:tile + CARRY, cols]

    for hd in range(GDN_HEADS):
        for base, scale in ((0, GDN_DK ** -0.5), (GDN_QK, 1.0)):
            cols = slice(base + hd * GDN_DK, base + (hd + 1) * GDN_DK)
            v = act_s[:, cols]
            act_s[:, cols] = v * (lax.rsqrt(jnp.sum(v * v, axis=-1, keepdims=True) + EPS) * scale)

    small = jnp.dot(hb, win_ref[:, COL_SMALL:COL_SMALL + LANES], preferred_element_type=F32)
    lane1 = lax.broadcasted_iota(jnp.int32, (1, LANES), 1)
    a_row = jnp.where(lane1 < N_GATE_LANES, -jnp.exp(sp_ref[1:2, :]), 0.0)
    sp = _softplus(small + sp_ref[0:1, :])
    col_s[0] = sp * a_row
    col_s[1] = sp
    col_s[2] = pltpu.roll(_sigmoid(small), LANES - LANE_BETA, axis=1)

    z_s[:, 0:GDN_V] = jnp.dot(hb, win_ref[:, COL_ZA:COL_ZA + GDN_V], preferred_element_type=F32)
    z_s[:, GDN_V:MIX_WIDTH] = jnp.dot(hb, win_ref[:, COL_ZS:COL_ZS + SSD_D], preferred_element_type=F32)

    r_cc = lax.broadcasted_iota(jnp.int32, (CHUNK, CHUNK), 0)
    c_cc = lax.broadcasted_iota(jnp.int32, (CHUNK, CHUNK), 1)
    tril = r_cc >= c_cc
    strict = r_cc > c_cc
    blk_shift = BLK.bit_length() - 1
    blk_mask = (r_cc >> blk_shift) == (c_cc >> blk_shift)
    eye = (r_cc == c_cc).astype(F32)
    lane_cl = lax.broadcasted_iota(jnp.int32, (CHUNK, LANES), 1)
    lane_sl = lax.broadcasted_iota(jnp.int32, (SSD_STATE, LANES), 1)
    neg_inf = -jnp.inf

    def chunk_body(c, carry):
        r0 = pl.multiple_of(c * CHUNK, CHUNK)
        rows = pl.ds(r0, CHUNK)
        gad = col_s[0, rows, :]
        dtv = col_s[1, rows, :]
        beta0 = col_s[2, rows, :]
        cs = _chunk_cumsum(gad)
        last = cs[CHUNK - 1:CHUNK, :]
        ecs = jnp.exp(cs)
        edl = jnp.exp(last - cs)
        gl = jnp.exp(last)
        is_g = lane_cl < GDN_HEADS
        cs_t = cs.T
        bd_t = jnp.where(is_g, beta0, dtv).T
        be_t = jnp.where(is_g, beta0 * ecs, edl * dtv).T
        edl_t = edl.T

        cb, b_t, c_g = [], [], []
        for g in range(SSD_GROUPS):
            bm = act_s[rows, COL_B + g * SSD_STATE:COL_B + (g + 1) * SSD_STATE]
            cm = act_s[rows, COL_C + g * SSD_STATE:COL_C + (g + 1) * SSD_STATE]
            cb.append(_dot_nt(cm, bm))
            b_t.append(bm.T)
            c_g.append(cm)
        n_pairs = SSD_HEADS // 2
        pairs_per_group = n_pairs // SSD_GROUPS
        xp = [_bf(act_s[rows, COL_XS + p * LANES:COL_XS + (p + 1) * LANES]) for p in range(n_pairs)]
        sst = [sst_s[p] for p in range(n_pairs)]
        sstb = [_bf(s) for s in sst]
        ys = [None] * SSD_HEADS
        sn = [None] * SSD_HEADS

        def y_job(i):
            p, g, ln = i // 2, i // 2 // pairs_per_group, LANE_DT + i
            lmat = jnp.exp(jnp.where(tril, cs[:, ln:ln + 1] - cs_t[ln:ln + 1, :], neg_inf))
            mh = cb[g] * lmat * bd_t[ln:ln + 1, :]
            ce = c_g[g] * ecs[:, ln:ln + 1]
            ys[i] = _dot(mh, xp[p]) + _dot(ce, sstb[p])

        def s_job(i):
            p, g, ln = i // 2, i // 2 // pairs_per_group, LANE_DT + i
            sn[i] = sst[p] * gl[:, ln:ln + 1] + _dot(b_t[g] * be_t[ln:ln + 1, :], xp[p])

        jobs = [functools.partial(y_job, i) for i in range(SSD_HEADS)]
        jobs += [functools.partial(s_job, i) for i in range(SSD_HEADS)]

        def run_jobs(count):
            for _ in range(min(count, len(jobs))):
                jobs.pop(0)()

        hs = range(GDN_HEADS)
        q = [act_s[rows, hd * GDN_DK:(hd + 1) * GDN_DK] for hd in hs]
        k = [act_s[rows, GDN_QK + hd * GDN_DK:GDN_QK + (hd + 1) * GDN_DK] for hd in hs]
        v = [act_s[rows, 2 * GDN_QK + hd * GDN_DV:2 * GDN_QK + (hd + 1) * GDN_DV] for hd in hs]
        kb = [_bf(k[h]) for h in hs]
        kq = [_dot_nt(jnp.concatenate([kb[h], _bf(q[h])], axis=0), kb[h]) for h in hs]
        run_jobs(3)
        dec = [jnp.exp(jnp.where(tril, cs[:, h:h + 1] - cs_t[h:h + 1, :], neg_inf)) for h in hs]
        a = [jnp.where(strict, kq[h][0:CHUNK] * beta0[:, h:h + 1] * dec[h], 0.0) for h in hs]
        tinv = _unit_lower_inverse(a, eye, blk_mask, run_jobs)
        ecol = [ecs[:, h:h + 1] for h in hs]
        rhs = [jnp.concatenate([_bf(v[h]), _bf(k[h] * ecol[h])], axis=1) for h in hs]
        sol = [_dot(tinv[h] * bd_t[h:h + 1, :], rhs[h]) for h in hs]
        run_jobs(3)
        gst = [gst_s[h] for h in hs]
        wq = [jnp.concatenate([_bf(sol[h][:, GDN_DV:]), _bf(q[h] * ecol[h])], axis=0) for h in hs]
        ws = [_dot(wq[h], gst[h]) for h in hs]
        run_jobs(3)
        v_new = [_bf(sol[h][:, 0:GDN_DV] - ws[h][0:CHUNK]) for h in hs]
        lhs = [jnp.concatenate([_bf(kq[h][CHUNK:] * dec[h]), _bf(k[h].T * edl_t[h:h + 1, :])], axis=0) for h in hs]
        upd = [_dot(lhs[h], v_new[h]) for h in hs]
        run_jobs(len(jobs))
        for h in hs:
            oy_s[rows, h * GDN_DV:(h + 1) * GDN_DV] = ws[h][CHUNK:] + upd[h][0:CHUNK]
            gst_s[h] = gst[h] * gl[:, h:h + 1] + upd[h][CHUNK:]
        for p in range(n_pairs):
            oy_s[rows, GDN_V + p * LANES:GDN_V + (p + 1) * LANES] = jnp.where(
                lane_cl < SSD_HEADDIM, ys[2 * p], ys[2 * p + 1])
            sst_s[p] = jnp.where(lane_sl < SSD_HEADDIM, sn[2 * p], sn[2 * p + 1])
        return carry

    lax.fori_loop(0, n_chunks, chunk_body, 0)

    for hd in range(GDN_HEADS):
        cols = slice(hd * GDN_DV, (hd + 1) * GDN_DV)
        o = oy_s[:, cols]
        o = o * lax.rsqrt(jnp.mean(o * o, axis=-1, keepdims=True) + EPS) * gnw_ref[:, cols]
        cat_s[:, cols] = _bf(o * _silu(z_s[:, cols]))
    gw = SSD_D // SSD_GROUPS
    for g in range(SSD_GROUPS):
        cols = slice(GDN_V + g * gw, GDN_V + (g + 1) * gw)
        xs = act_s[:, COL_XS + g * gw:COL_XS + (g + 1) * gw]
        y = oy_s[:, cols] + dsk_ref[:, g * gw:(g + 1) * gw] * xs
        y = y * _silu(z_s[:, cols])
        y = y * lax.rsqrt(jnp.mean(y * y, axis=-1, keepdims=True) + EPS) * snw_ref[:, g * gw:(g + 1) * gw]
        cat_s[:, cols] = _bf(y)

    mix = jnp.dot(cat_s[...], wout_ref[...], preferred_element_type=F32)
    o_ref[0] = x + _rmsnorm(mix, pnw_ref[...])


def _ffn_kernel(x_ref, nw_ref, wup_ref, cw_ref, cb_ref, wdn_ref, pnw_ref, o_ref, ub_s, carry_s, act_s):
    tile = x_ref.shape[1]
    j = pl.program_id(1)

    @pl.when(j == 0)
    def _():
        carry_s[...] = jnp.zeros(carry_s.shape, F32)

    x = x_ref[0]
    hb = _bf(_rmsnorm(x, nw_ref[...]))
    first = CARRY - FFN_CONV_K + 1
    for c in range(D_FF // FFN_CW):
        slot = c % 2
        halves = []
        for base in (0, D_FF):
            cols = slice(base + c * FFN_CW, base + (c + 1) * FFN_CW)
            sc = slice(0, FFN_CW) if base == 0 else slice(FFN_CW, 2 * FFN_CW)
            ub_s[slot, 0:CARRY, sc] = carry_s[:, cols]
            ub_s[slot, CARRY:CARRY + tile, sc] = jnp.dot(hb, wup_ref[:, cols], preferred_element_type=F32)
            acc = cb_ref[:, cols] + cw_ref[0:1, cols] * ub_s[slot, first:first + tile, sc]
            for k in range(1, FFN_CONV_K):
                acc = acc + cw_ref[k:k + 1, cols] * ub_s[slot, first + k:first + k + tile, sc]
            carry_s[:, cols] = ub_s[slot, tile:tile + CARRY, sc]
            halves.append(acc)
        act_s[:, c * FFN_CW:(c + 1) * FFN_CW] = _bf(_silu(halves[0]) * halves[1])
    f = jnp.dot(act_s[...], wdn_ref[...], preferred_element_type=F32)
    o_ref[0] = x + _rmsnorm(f, pnw_ref[...])


def _const_spec(shape):
    return pl.BlockSpec(shape, lambda b, j: (0,) * len(shape), pipeline_mode=pl.Buffered(1))


def _mixer_call(x, nw, win, cw, cb, sp, gnw, dsk, snw, wout, pnw):
    bsz, seq, _ = x.shape
    tile = MIX_TILE
    assert seq % tile == 0 and tile % CHUNK == 0
    tok_spec = pl.BlockSpec((1, tile, D_MODEL), lambda b, j: (b, j, 0))
    consts = (nw, win, cw, cb, sp, gnw, dsk, snw, wout, pnw)
    return pl.pallas_call(
        _mixer_kernel,
        grid=(bsz, seq // tile),
        in_specs=[tok_spec] + [_const_spec(c.shape) for c in consts],
        out_specs=tok_spec,
        out_shape=jax.ShapeDtypeStruct(x.shape, x.dtype),
        scratch_shapes=[
            pltpu.VMEM((tile, D_MODEL), BF16),
            pltpu.VMEM((CARRY + tile, CONV_CH), F32),
            pltpu.VMEM((tile, CONV_CH), F32),
            pltpu.VMEM((3, tile, LANES), F32),
            pltpu.VMEM((tile, MIX_WIDTH), F32),
            pltpu.VMEM((tile, MIX_WIDTH), F32),
            pltpu.VMEM((tile, MIX_WIDTH), BF16),
            pltpu.VMEM((GDN_HEADS, GDN_DK, GDN_DV), F32),
            pltpu.VMEM((SSD_HEADS // 2, SSD_STATE, LANES), F32),
        ],
        compiler_params=pltpu.CompilerParams(
            dimension_semantics=("arbitrary", "arbitrary"), vmem_limit_bytes=VMEM_LIMIT_BYTES),
        name="mixer",
    )(x, *consts)


def _ffn_call(x, nw, wup, cw, cb, wdn, pnw):
    bsz, seq, _ = x.shape
    tile = FFN_TILE
    assert seq % tile == 0
    tok_spec = pl.BlockSpec((1, tile, D_MODEL), lambda b, j: (b, j, 0))
    consts = (nw, wup, cw, cb, wdn, pnw)
    return pl.pallas_call(
        _ffn_kernel,
        grid=(bsz, seq // tile),
        in_specs=[tok_spec] + [_const_spec(c.shape) for c in consts],
        out_specs=tok_spec,
        out_shape=jax.ShapeDtypeStruct(x.shape, x.dtype),
        scratch_shapes=[
            pltpu.VMEM((2, CARRY + tile, 2 * FFN_CW), F32),
            pltpu.VMEM((CARRY, 2 * D_FF), F32),
            pltpu.VMEM((tile, D_FF), BF16),
        ],
        compiler_params=pltpu.CompilerParams(
            dimension_semantics=("arbitrary", "arbitrary"), vmem_limit_bytes=VMEM_LIMIT_BYTES),
        name="ffn",
    )(x, *consts)


def _row(v):
    return v.reshape(1, -1).astype(F32)


def _layer(x, pre_mix_norm, w_in, gdn_conv_w, gdn_a_log, gdn_dt_bias, gdn_norm_w, ssd_conv_w, ssd_conv_b,
           ssd_a_log, ssd_dt_bias, ssd_d, ssd_norm_w, w_out, post_mix_norm, pre_ffn_norm, w_up, ffn_conv_w,
           ffn_conv_b, w_down, post_ffn_norm):
    offs = [0]
    for n in IN_SPLITS:
        offs.append(offs[-1] + n)
    seg = [w_in[:, offs[i]:offs[i + 1]] for i in range(len(IN_SPLITS))]
    wq, wk, wv, wza, wb, wa, wzs, wxs, wbs, wcs, wdt = seg
    pad = jnp.zeros((D_MODEL, LANES - N_GATE_LANES - GDN_HEADS), w_in.dtype)
    win = _bf(jnp.concatenate([wq, wk, wv, wxs, wbs, wcs, wza, wzs, wa, wdt, wb, pad], axis=1))
    cw = jnp.concatenate([gdn_conv_w, ssd_conv_w], axis=1).astype(F32)
    cb = jnp.concatenate([jnp.zeros((GDN_CONV_CH,), F32), ssd_conv_b.astype(F32)]).reshape(1, CONV_CH)
    zpad = jnp.zeros((LANES - N_GATE_LANES,), F32)
    sp = jnp.zeros((SUBLANES, LANES), F32)
    sp = sp.at[0].set(jnp.concatenate([gdn_dt_bias.astype(F32), ssd_dt_bias.astype(F32), zpad]))
    sp = sp.at[1].set(jnp.concatenate([gdn_a_log.astype(F32), ssd_a_log.astype(F32), zpad]))
    gnw = _row(jnp.tile(gdn_norm_w, GDN_HEADS))
    dsk = _row(jnp.repeat(ssd_d, SSD_HEADDIM))
    x1 = _mixer_call(x, _row(pre_mix_norm), win, cw, cb, sp, gnw, dsk, _row(ssd_norm_w), _bf(w_out),
                     _row(post_mix_norm))
    return _ffn_call(x1, _row(pre_ffn_norm), _bf(w_up), ffn_conv_w.astype(F32), _row(ffn_conv_b), _bf(w_down),
                     _row(post_ffn_norm))


def kernel(x, pre_mix_norm, w_in, gdn_conv_w, gdn_a_log, gdn_dt_bias, gdn_norm_w, ssd_conv_w, ssd_conv_b, ssd_a_log, ssd_dt_bias, ssd_d, ssd_norm_w, w_out, post_mix_norm, pre_ffn_norm, w_up, ffn_conv_w, ffn_conv_b, w_down, post_ffn_norm):
    params = (pre_mix_norm, w_in, gdn_conv_w, gdn_a_log, gdn_dt_bias, gdn_norm_w, ssd_conv_w, ssd_conv_b, ssd_a_log,
              ssd_dt_bias, ssd_d, ssd_norm_w, w_out, post_mix_norm, pre_ffn_norm, w_up, ffn_conv_w, ffn_conv_b,
              w_down, post_ffn_norm)
    for layer in range(pre_mix_norm.shape[0]):
        x = _layer(x, *(p[layer] for p in params))
    return x
```

```python
import functools

import jax
import jax.numpy as jnp
import numpy as np
from jax import lax
from jax.experimental import pallas as pl
from jax.experimental.pallas import tpu as pltpu

F32 = jnp.float32
BF16 = jnp.bfloat16

D_MODEL = 1024
GDN_HEADS = 8
GDN_DK = 128
GDN_DV = 128
SSD_HEADS = 16
SSD_HEADDIM = 64
SSD_GROUPS = 2
SSD_STATE = 128
CONV_K = 4
CHUNK = 64
D_FF = 2816
FFN_CONV_K = 3
EPS = 1e-6

GDN_QK = GDN_HEADS * GDN_DK
GDN_V = GDN_HEADS * GDN_DV
SSD_D = SSD_HEADS * SSD_HEADDIM
SSD_BC = SSD_GROUPS * SSD_STATE
MIX_WIDTH = GDN_V + SSD_D
GDN_CONV_CH = 2 * GDN_QK + GDN_V
SSD_CONV_CH = SSD_D + 2 * SSD_BC
CONV_CH = GDN_CONV_CH + SSD_CONV_CH
IN_SPLITS = (GDN_QK, GDN_QK, GDN_V, GDN_V, GDN_HEADS, GDN_HEADS, SSD_D, SSD_D, SSD_BC, SSD_BC, SSD_HEADS)

LANES = 128
SUBLANES = 8
VMEM_LIMIT_BYTES = 56 * 1024 * 1024

COL_XS = GDN_CONV_CH
COL_B = COL_XS + SSD_D
COL_C = COL_B + SSD_BC
COL_ZA = CONV_CH
COL_ZS = COL_ZA + GDN_V
COL_SMALL = COL_ZS + SSD_D
W_IN_COLS = COL_SMALL + LANES
LANE_G = 0
LANE_DT = GDN_HEADS
LANE_BETA = GDN_HEADS + SSD_HEADS
N_GATE_LANES = GDN_HEADS + SSD_HEADS

MIX_TILE = 256
FFN_TILE = 512
PERM_BLOCK = 256
MIX_CW = 256
FFN_CW = 256
FFN_SUB = 2
GROUP = 4
BLK = 16
RB = CHUNK // SUBLANES
assert RB == SUBLANES


def _bf(x):
    return x.astype(BF16)


def _dot(a, b):
    return jnp.dot(_bf(a), _bf(b), preferred_element_type=F32)


def _dot_nt(a, b):
    return lax.dot_general(_bf(a), _bf(b), (((1,), (1,)), ((), ())), preferred_element_type=F32)


def _rmsnorm(x, w):
    return x * lax.rsqrt(jnp.mean(x * x, axis=-1, keepdims=True) + EPS) * w


def _sigmoid(x):
    return 1.0 / (1.0 + jnp.exp(-x))


def _silu(x):
    return x * _sigmoid(x)


def _softplus(x):
    return jnp.maximum(x, 0.0) + jnp.log1p(jnp.exp(-jnp.abs(x)))


def _row_time(idx):
    return ((idx & (SUBLANES - 1)) << 3) | (idx >> 3)


def _perm_matrix(rows):
    r = np.arange(rows)
    t = (r // CHUNK) * CHUNK + ((r % SUBLANES) * RB + (r % CHUNK) // SUBLANES)
    p = np.zeros((rows, rows), np.float32)
    p[r, t] = 1.0
    return jnp.asarray(p, BF16)


def _to_chunk_order(hb, perm_ref):
    tile = hb.shape[0]
    parts = [jnp.dot(perm_ref[...], hb[r:r + PERM_BLOCK], preferred_element_type=F32).astype(BF16)
             for r in range(0, tile, PERM_BLOCK)]
    return parts[0] if len(parts) == 1 else jnp.concatenate(parts, axis=0)


def _store_natural(o_ref, x_ref, rn, slab_s, row0):
    n = rn.shape[0]
    for lt in range(D_MODEL // LANES):
        slab_s[lt, row0:row0 + n, :] = rn[:, lt * LANES:(lt + 1) * LANES]
    for lt in range(D_MODEL // LANES):
        cols = slice(lt * LANES, (lt + 1) * LANES)
        for c in range(n // CHUNK):
            for s in range(SUBLANES):
                r0 = row0 + c * CHUNK + s * RB
                o_ref[0, r0:r0 + RB, cols] = (x_ref[0, r0:r0 + RB, cols]
                                              + slab_s[lt, pl.ds(row0 + c * CHUNK + s, RB, stride=SUBLANES), :])


def _conv_chunks(streams, carry_ref):
    tile = streams[0][0].shape[0]
    nk = len(streams[0][1])
    sub = lax.broadcasted_iota(jnp.int32, (SUBLANES, streams[0][0].shape[1]), 0)
    prev = [[carry_ref[j * SUBLANES:(j + 1) * SUBLANES, cols] for j in range(nk - 1)] for _, _, _, cols in streams]
    for c in range(tile // CHUNK):
        exts = []
        for si, (u, _, _, _) in enumerate(streams):
            x = [u[c * CHUNK + i * SUBLANES:c * CHUNK + (i + 1) * SUBLANES] for i in range(RB)]
            z = [pltpu.roll(jnp.where(sub == SUBLANES - 1, prev[si][j], x[RB - (nk - 1) + j]), 1, axis=0)
                 for j in range(nk - 1)]
            exts.append(z + x)
            prev[si] = x[RB - (nk - 1):]
        for i in range(RB):
            accs = []
            for (_, taps, bias, _), ext in zip(streams, exts):
                acc = bias + taps[0] * ext[i]
                for k in range(1, nk):
                    acc = acc + taps[k] * ext[i + k]
                accs.append(acc)
            yield c * RB + i, accs
    for (_, _, _, cols), pv in zip(streams, prev):
        for j in range(nk - 1):
            carry_ref[j * SUBLANES:(j + 1) * SUBLANES, cols] = pv[j]


def _chunk_cumsum(x):
    p = [x[0:SUBLANES]]
    for i in range(1, RB):
        p.append(p[-1] + x[i * SUBLANES:(i + 1) * SUBLANES])
    tot = p[-1]
    sub = lax.broadcasted_iota(jnp.int32, tot.shape, 0)
    inc = tot
    s = 1
    while s < SUBLANES:
        inc = inc + jnp.where(sub >= s, pltpu.roll(inc, s, axis=0), 0.0)
        s *= 2
    exc = inc - tot
    return jnp.concatenate([pi + exc for pi in p], axis=0)


def _unit_lower_inverse(a, eye, blk_mask, fill):
    hs = range(len(a))
    d = [jnp.where(blk_mask, a[h], 0.0) for h in hs]
    n = [_bf(a[h] - d[h]) for h in hs]
    m = [eye - d[h] for h in hs]
    p = [_bf(d[h]) for h in hs]
    p = [_dot(p[h], p[h]) for h in hs]
    fill()
    for _ in range(2):
        m = [m[h] + _dot(m[h], p[h]) for h in hs]
        p = [_bf(p[h]) for h in hs]
        p = [_dot(p[h], p[h]) for h in hs]
        fill()
    m = [_bf(m[h] + _dot(m[h], p[h])) for h in hs]
    fill()
    e = [_dot(m[h], n[h]) for h in hs]
    fill()
    f = [eye - e[h] for h in hs]
    e = [_bf(e[h]) for h in hs]
    e = [_dot(e[h], e[h]) for h in hs]
    fill()
    f = [f[h] + _dot(f[h], e[h]) for h in hs]
    fill()
    t = [_dot(f[h], m[h]) for h in hs]
    fill()
    return t


def _mixer_kernel(x_ref, perm_ref, nw_ref, win_ref, cw_ref, cb_ref, sp_ref, gnw_ref, dsk_ref, snw_ref, wout_ref,
                  pnw_ref, o_ref, h_s, carry_s, act_s, col_s, oy_s, z_s, cat_s, slab_s, gst_s, sst_s):
    tile = x_ref.shape[1]
    n_groups = tile // (CHUNK * GROUP)

    @pl.when(pl.program_id(1) == 0)
    def _():
        carry_s[...] = jnp.zeros(carry_s.shape, F32)
        gst_s[...] = jnp.zeros(gst_s.shape, F32)
        sst_s[...] = jnp.zeros(sst_s.shape, F32)

    h_s[...] = _to_chunk_order(_bf(_rmsnorm(x_ref[0], nw_ref[...])), perm_ref)

    for cbk in range(CONV_CH // MIX_CW):
        cols = slice(cbk * MIX_CW, (cbk + 1) * MIX_CW)
        pre = jnp.dot(h_s[...], win_ref[:, cols], preferred_element_type=F32)
        taps = [jnp.broadcast_to(cw_ref[k:k + 1, cols], (SUBLANES, MIX_CW)) for k in range(CONV_K)]
        bias = jnp.broadcast_to(cb_ref[:, cols], (SUBLANES, MIX_CW))
        if cbk * MIX_CW < 2 * GDN_QK:
            scale = GDN_DK ** -0.5 if cbk * MIX_CW < GDN_QK else 1.0
        else:
            scale = None
        for r, (acc,) in _conv_chunks([(pre, taps, bias, cols)], carry_s):
            a = _silu(acc)
            if scale is not None:
                parts = []
                for hh in range(MIX_CW // GDN_DK):
                    v = a[:, hh * GDN_DK:(hh + 1) * GDN_DK]
                    parts.append(v * (lax.rsqrt(jnp.sum(v * v, axis=-1, keepdims=True) + EPS) * scale))
                a = jnp.concatenate(parts, axis=1)
            act_s[r * SUBLANES:(r + 1) * SUBLANES, cols] = a

    small = jnp.dot(h_s[...], win_ref[:, COL_SMALL:COL_SMALL + LANES], preferred_element_type=F32)
    lane1 = lax.broadcasted_iota(jnp.int32, (1, LANES), 1)
    a_row = jnp.where(lane1 < N_GATE_LANES, -jnp.exp(sp_ref[1:2, :]), 0.0)
    sp = _softplus(small + sp_ref[0:1, :])
    col_s[0] = sp * a_row
    col_s[1] = sp
    col_s[2] = pltpu.roll(_sigmoid(small), LANES - LANE_BETA, axis=1)

    z_s[:, 0:GDN_V] = jnp.dot(h_s[...], win_ref[:, COL_ZA:COL_ZA + GDN_V], preferred_element_type=F32)
    z_s[:, GDN_V:MIX_WIDTH] = jnp.dot(h_s[...], win_ref[:, COL_ZS:COL_ZS + SSD_D], preferred_element_type=F32)

    r_t = _row_time(lax.broadcasted_iota(jnp.int32, (CHUNK, CHUNK), 0))
    c_t = _row_time(lax.broadcasted_iota(jnp.int32, (CHUNK, CHUNK), 1))
    tril = r_t >= c_t
    strict = r_t > c_t
    blk_shift = BLK.bit_length() - 1
    blk_mask = (r_t >> blk_shift) == (c_t >> blk_shift)
    eye = (r_t == c_t).astype(F32)
    lane_cl = lax.broadcasted_iota(jnp.int32, (CHUNK, LANES), 1)
    lane_sl = lax.broadcasted_iota(jnp.int32, (SSD_STATE, LANES), 1)
    is_g = lane_cl < GDN_HEADS
    neg_inf = -jnp.inf
    n_pairs = SSD_HEADS // 2
    pairs_per_group = n_pairs // SSD_GROUPS
    hs = range(GDN_HEADS)

    def group_body(gi, carry):
        base = gi * (CHUNK * GROUP)
        rows = [pl.ds(pl.multiple_of(base + cc * CHUNK, CHUNK), CHUNK) for cc in range(GROUP)]

        cs, ecs, gl, beta0, cs_t, bd_t, be_t, edl_t = [], [], [], [], [], [], [], []
        for cc in range(GROUP):
            gad = col_s[0, rows[cc], :]
            dtv = col_s[1, rows[cc], :]
            b0 = col_s[2, rows[cc], :]
            c_ = _chunk_cumsum(gad)
            last = c_[CHUNK - 1:CHUNK, :]
            e_ = jnp.exp(c_)
            edl = jnp.exp(last - c_)
            cs.append(c_)
            ecs.append(e_)
            gl.append(jnp.exp(last))
            beta0.append(b0)
            cs_t.append(c_.T)
            bd_t.append(jnp.where(is_g, b0, dtv).T)
            be_t.append(jnp.where(is_g, b0 * e_, edl * dtv).T)
            edl_t.append(edl.T)

        sst = [sst_s[p] for p in range(n_pairs)]
        sstb = [_bf(s) for s in sst]
        jobs = []
        for cc in range(GROUP):
            cb, b_t, c_g = [], [], []
            ys = [None] * SSD_HEADS
            sn = [None] * SSD_HEADS
            xp = [None] * n_pairs

            def prep_job(cc=cc, cb=cb, b_t=b_t, c_g=c_g, xp=xp):
                for g in range(SSD_GROUPS):
                    bm = act_s[rows[cc], COL_B + g * SSD_STATE:COL_B + (g + 1) * SSD_STATE]
                    cm = act_s[rows[cc], COL_C + g * SSD_STATE:COL_C + (g + 1) * SSD_STATE]
                    cb.append(_dot_nt(cm, bm))
                    b_t.append(bm.T)
                    c_g.append(cm)
                for p in range(n_pairs):
                    xp[p] = _bf(act_s[rows[cc], COL_XS + p * LANES:COL_XS + (p + 1) * LANES])

            def y_job(i, cc=cc, cb=cb, c_g=c_g, xp=xp, ys=ys):
                p, g, ln = i // 2, i // 2 // pairs_per_group, LANE_DT + i
                lmat = jnp.exp(jnp.where(tril, cs[cc][:, ln:ln + 1] - cs_t[cc][ln:ln + 1, :], neg_inf))
                mh = cb[g] * lmat * bd_t[cc][ln:ln + 1, :]
                ce = c_g[g] * ecs[cc][:, ln:ln + 1]
                ys[i] = _dot(mh, xp[p]) + _dot(ce, sstb[p])

            def s_job(i, cc=cc, b_t=b_t, xp=xp, sn=sn):
                p, g, ln = i // 2, i // 2 // pairs_per_group, LANE_DT + i
                sn[i] = sst[p] * gl[cc][:, ln:ln + 1] + _dot(b_t[g] * be_t[cc][ln:ln + 1, :], xp[p])

            def fin_job(p, cc=cc, ys=ys, sn=sn):
                oy_s[rows[cc], GDN_V + p * LANES:GDN_V + (p + 1) * LANES] = jnp.where(
                    lane_cl < SSD_HEADDIM, ys[2 * p], ys[2 * p + 1])
                sst[p] = jnp.where(lane_sl < SSD_HEADDIM, sn[2 * p], sn[2 * p + 1])
                sstb[p] = _bf(sst[p])

            jobs.append(prep_job)
            jobs += [functools.partial(y_job, i) for i in range(SSD_HEADS)]
            for p in range(n_pairs):
                jobs += [functools.partial(s_job, 2 * p), functools.partial(s_job, 2 * p + 1),
                         functools.partial(fin_job, p)]
        n_slots = 10 + 2 * GROUP
        per_slot = -(-len(jobs) // n_slots)

        def fill():
            for _ in range(min(per_slot, len(jobs))):
                jobs.pop(0)()

        ch = [(cc, h) for cc in range(GROUP) for h in hs]
        nch = range(len(ch))
        q = [act_s[rows[cc], h * GDN_DK:(h + 1) * GDN_DK] for cc, h in ch]
        k = [act_s[rows[cc], GDN_QK + h * GDN_DK:GDN_QK + (h + 1) * GDN_DK] for cc, h in ch]
        v = [act_s[rows[cc], 2 * GDN_QK + h * GDN_DV:2 * GDN_QK + (h + 1) * GDN_DV] for cc, h in ch]
        kb = [_bf(k[i]) for i in nch]
        kq = [_dot_nt(jnp.concatenate([kb[i], _bf(q[i])], axis=0), kb[i]) for i in nch]
        fill()
        dec = [jnp.exp(jnp.where(tril, cs[cc][:, h:h + 1] - cs_t[cc][h:h + 1, :], neg_inf)) for cc, h in ch]
        a = [jnp.where(strict, kq[i][0:CHUNK] * beta0[cc][:, h:h + 1] * dec[i], 0.0) for i, (cc, h) in enumerate(ch)]
        tinv = _unit_lower_inverse(a, eye, blk_mask, fill)
        ecol = [ecs[cc][:, h:h + 1] for cc, h in ch]
        rhs = [jnp.concatenate([_bf(v[i]), _bf(k[i] * ecol[i])], axis=1) for i in nch]
        sol = [_dot(tinv[i] * bd_t[cc][h:h + 1, :], rhs[i]) for i, (cc, h) in enumerate(ch)]
        fill()
        qe = [_bf(q[i] * ecol[i]) for i in nch]
        qkd = [_bf(kq[i][CHUNK:] * dec[i]) for i in nch]
        kdt = [_bf(k[i].T * edl_t[cc][h:h + 1, :]) for i, (cc, h) in enumerate(ch)]

        gst = [gst_s[h] for h in hs]
        for cc in range(GROUP):
            idx = [cc * GDN_HEADS + h for h in hs]
            wq = [jnp.concatenate([_bf(sol[i][:, GDN_DV:]), qe[i]], axis=0) for i in idx]
            ws = [_dot(wq[h], gst[h]) for h in hs]
            fill()
            v_new = [_bf(sol[i][:, 0:GDN_DV] - ws[h][0:CHUNK]) for h, i in enumerate(idx)]
            lhs = [jnp.concatenate([qkd[i], kdt[i]], axis=0) for i in idx]
            upd = [_dot(lhs[h], v_new[h]) for h in hs]
            fill()
            for h in hs:
                oy_s[rows[cc], h * GDN_DV:(h + 1) * GDN_DV] = ws[h][CHUNK:] + upd[h][0:CHUNK]
                gst[h] = gst[h] * gl[cc][:, h:h + 1] + upd[h][CHUNK:]
        while jobs:
            jobs.pop(0)()
        for h in hs:
            gst_s[h] = gst[h]
        for p in range(n_pairs):
            sst_s[p] = sst[p]
        return carry

    lax.fori_loop(0, n_groups, group_body, 0)

    for hd in range(GDN_HEADS):
        cols = slice(hd * GDN_DV, (hd + 1) * GDN_DV)
        o = oy_s[:, cols]
        o = o * lax.rsqrt(jnp.mean(o * o, axis=-1, keepdims=True) + EPS) * gnw_ref[:, cols]
        cat_s[:, cols] = _bf(o * _silu(z_s[:, cols]))
    gw = SSD_D // SSD_GROUPS
    for g in range(SSD_GROUPS):
        cols = slice(GDN_V + g * gw, GDN_V + (g + 1) * gw)
        xs = act_s[:, COL_XS + g * gw:COL_XS + (g + 1) * gw]
        y = oy_s[:, cols] + dsk_ref[:, g * gw:(g + 1) * gw] * xs
        y = y * _silu(z_s[:, cols])
        y = y * lax.rsqrt(jnp.mean(y * y, axis=-1, keepdims=True) + EPS) * snw_ref[:, g * gw:(g + 1) * gw]
        cat_s[:, cols] = _bf(y)

    mix = jnp.dot(cat_s[...], wout_ref[...], preferred_element_type=F32)
    _store_natural(o_ref, x_ref, _rmsnorm(mix, pnw_ref[...]), slab_s, 0)


def _ffn_kernel(x_ref, perm_ref, nw_ref, wup_ref, cw_ref, cb_ref, wdn_ref, pnw_ref, o_ref, h_s, carry_s, act_s,
                slab_s):
    tile = x_ref.shape[1]
    sub_rows = tile // FFN_SUB

    @pl.when(pl.program_id(1) == 0)
    def _():
        carry_s[...] = jnp.zeros(carry_s.shape, F32)

    def prenorm(sub):
        rows = slice(sub * sub_rows, (sub + 1) * sub_rows)
        h_s[rows, :] = _to_chunk_order(_bf(_rmsnorm(x_ref[0, rows, :], nw_ref[...])), perm_ref)

    prenorm(0)
    for sub in range(FFN_SUB):
        rows = slice(sub * sub_rows, (sub + 1) * sub_rows)
        for c in range(D_FF // FFN_CW):
            streams = []
            for base in (0, D_FF):
                cols = slice(base + c * FFN_CW, base + (c + 1) * FFN_CW)
                u = jnp.dot(h_s[rows, :], wup_ref[:, cols], preferred_element_type=F32)
                taps = [jnp.broadcast_to(cw_ref[k:k + 1, cols], (SUBLANES, FFN_CW)) for k in range(FFN_CONV_K)]
                streams.append((u, taps, jnp.broadcast_to(cb_ref[:, cols], (SUBLANES, FFN_CW)), cols))
            for r, (g, up) in _conv_chunks(streams, carry_s):
                r0 = sub * sub_rows + r * SUBLANES
                act_s[r0:r0 + SUBLANES, c * FFN_CW:(c + 1) * FFN_CW] = _bf(_silu(g) * up)
        if sub + 1 < FFN_SUB:
            prenorm(sub + 1)
        f = jnp.dot(act_s[rows, :], wdn_ref[...], preferred_element_type=F32)
        _store_natural(o_ref, x_ref, _rmsnorm(f, pnw_ref[...]), slab_s, sub * sub_rows)


def _const_spec(shape):
    return pl.BlockSpec(shape, lambda b, j: (0,) * len(shape), pipeline_mode=pl.Buffered(1))


def _mixer_call(x, nw, win, cw, cb, sp, gnw, dsk, snw, wout, pnw):
    bsz, seq, _ = x.shape
    tile = MIX_TILE
    assert seq % tile == 0 and tile % (CHUNK * GROUP) == 0 and tile % PERM_BLOCK == 0
    tok_spec = pl.BlockSpec((1, tile, D_MODEL), lambda b, j: (b, j, 0))
    consts = (_perm_matrix(PERM_BLOCK), nw, win, cw, cb, sp, gnw, dsk, snw, wout, pnw)
    return pl.pallas_call(
        _mixer_kernel,
        grid=(bsz, seq // tile),
        in_specs=[tok_spec] + [_const_spec(c.shape) for c in consts],
        out_specs=tok_spec,
        out_shape=jax.ShapeDtypeStruct(x.shape, x.dtype),
        scratch_shapes=[
            pltpu.VMEM((tile, D_MODEL), BF16),
            pltpu.VMEM(((CONV_K - 1) * SUBLANES, CONV_CH), F32),
            pltpu.VMEM((tile, CONV_CH), F32),
            pltpu.VMEM((3, tile, LANES), F32),
            pltpu.VMEM((tile, MIX_WIDTH), F32),
            pltpu.VMEM((tile, MIX_WIDTH), F32),
            pltpu.VMEM((tile, MIX_WIDTH), BF16),
            pltpu.VMEM((D_MODEL // LANES, tile, LANES), F32),
            pltpu.VMEM((GDN_HEADS, GDN_DK, GDN_DV), F32),
            pltpu.VMEM((SSD_HEADS // 2, SSD_STATE, LANES), F32),
        ],
        compiler_params=pltpu.CompilerParams(
            dimension_semantics=("arbitrary", "arbitrary"), vmem_limit_bytes=VMEM_LIMIT_BYTES),
        name="mixer",
    )(x, *consts)


def _ffn_call(x, nw, wup, cw, cb, wdn, pnw):
    bsz, seq, _ = x.shape
    tile = FFN_TILE
    assert seq % tile == 0 and tile % (FFN_SUB * PERM_BLOCK) == 0
    tok_spec = pl.BlockSpec((1, tile, D_MODEL), lambda b, j: (b, j, 0))
    consts = (_perm_matrix(PERM_BLOCK), nw, wup, cw, cb, wdn, pnw)
    return pl.pallas_call(
        _ffn_kernel,
        grid=(bsz, seq // tile),
        in_specs=[tok_spec] + [_const_spec(c.shape) for c in consts],
        out_specs=tok_spec,
        out_shape=jax.ShapeDtypeStruct(x.shape, x.dtype),
        scratch_shapes=[
            pltpu.VMEM((tile, D_MODEL), BF16),
            pltpu.VMEM(((FFN_CONV_K - 1) * SUBLANES, 2 * D_FF), F32),
            pltpu.VMEM((tile, D_FF), BF16),
            pltpu.VMEM((D_MODEL // LANES, tile, LANES), F32),
        ],
        compiler_params=pltpu.CompilerParams(
            dimension_semantics=("arbitrary", "arbitrary"), vmem_limit_bytes=VMEM_LIMIT_BYTES),
        name="ffn",
    )(x, *consts)


def _row(v):
    return v.reshape(1, -1).astype(F32)


def _layer(x, pre_mix_norm, w_in, gdn_conv_w, gdn_a_log, gdn_dt_bias, gdn_norm_w, ssd_conv_w, ssd_conv_b,
           ssd_a_log, ssd_dt_bias, ssd_d, ssd_norm_w, w_out, post_mix_norm, pre_ffn_norm, w_up, ffn_conv_w,
           ffn_conv_b, w_down, post_ffn_norm):
    offs = [0]
    for n in IN_SPLITS:
        offs.append(offs[-1] + n)
    seg = [w_in[:, offs[i]:offs[i + 1]] for i in range(len(IN_SPLITS))]
    wq, wk, wv, wza, wb, wa, wzs, wxs, wbs, wcs, wdt = seg
    pad = jnp.zeros((D_MODEL, LANES - N_GATE_LANES - GDN_HEADS), w_in.dtype)
    win = _bf(jnp.concatenate([wq, wk, wv, wxs, wbs, wcs, wza, wzs, wa, wdt, wb, pad], axis=1))
    cw = jnp.concatenate([gdn_conv_w, ssd_conv_w], axis=1).astype(F32)
    cb = jnp.concatenate([jnp.zeros((GDN_CONV_CH,), F32), ssd_conv_b.astype(F32)]).reshape(1, CONV_CH)
    zpad = jnp.zeros((LANES - N_GATE_LANES,), F32)
    sp = jnp.zeros((SUBLANES, LANES), F32)
    sp = sp.at[0].set(jnp.concatenate([gdn_dt_bias.astype(F32), ssd_dt_bias.astype(F32), zpad]))
    sp = sp.at[1].set(jnp.concatenate([gdn_a_log.astype(F32), ssd_a_log.astype(F32), zpad]))
    gnw = _row(jnp.tile(gdn_norm_w, GDN_HEADS))
    dsk = _row(jnp.repeat(ssd_d, SSD_HEADDIM))
    x1 = _mixer_call(x, _row(pre_mix_norm), win, cw, cb, sp, gnw, dsk, _row(ssd_norm_w), _bf(w_out),
                     _row(post_mix_norm))
    return _ffn_call(x1, _row(pre_ffn_norm), _bf(w_up), ffn_conv_w.astype(F32), _row(ffn_conv_b), _bf(w_down),
                     _row(post_ffn_norm))


def kernel(x, pre_mix_norm, w_in, gdn_conv_w, gdn_a_log, gdn_dt_bias, gdn_norm_w, ssd_conv_w, ssd_conv_b, ssd_a_log, ssd_dt_bias, ssd_d, ssd_norm_w, w_out, post_mix_norm, pre_ffn_norm, w_up, ffn_conv_w, ffn_conv_b, w_down, post_ffn_norm):
    params = (pre_mix_norm, w_in, gdn_conv_w, gdn_a_log, gdn_dt_bias, gdn_norm_w, ssd_conv_w, ssd_conv_b, ssd_a_log,
              ssd_dt_bias, ssd_d, ssd_norm_w, w_out, post_mix_norm, pre_ffn_norm, w_up, ffn_conv_w, ffn_conv_b,
              w_down, post_ffn_norm)
    for layer in range(pre_mix_norm.shape[0]):
        x = _layer(x, *(p[layer] for p in params))
    return x
```

```python
import functools

import jax
import jax.numpy as jnp
import numpy as np
from jax import lax
from jax.experimental import pallas as pl
from jax.experimental.pallas import tpu as pltpu

F32 = jnp.float32
BF16 = jnp.bfloat16

D_MODEL = 1024
GDN_HEADS = 8
GDN_DK = 128
GDN_DV = 128
SSD_HEADS = 16
SSD_HEADDIM = 64
SSD_GROUPS = 2
SSD_STATE = 128
CONV_K = 4
CHUNK = 64
D_FF = 2816
FFN_CONV_K = 3
EPS = 1e-6

GDN_QK = GDN_HEADS * GDN_DK
GDN_V = GDN_HEADS * GDN_DV
SSD_D = SSD_HEADS * SSD_HEADDIM
SSD_BC = SSD_GROUPS * SSD_STATE
MIX_WIDTH = GDN_V + SSD_D
GDN_CONV_CH = 2 * GDN_QK + GDN_V
SSD_CONV_CH = SSD_D + 2 * SSD_BC
CONV_CH = GDN_CONV_CH + SSD_CONV_CH
IN_SPLITS = (GDN_QK, GDN_QK, GDN_V, GDN_V, GDN_HEADS, GDN_HEADS, SSD_D, SSD_D, SSD_BC, SSD_BC, SSD_HEADS)

LANES = 128
SUBLANES = 8
VMEM_LIMIT_BYTES = 56 * 1024 * 1024

COL_XS = GDN_CONV_CH
COL_B = COL_XS + SSD_D
COL_C = COL_B + SSD_BC
COL_ZA = CONV_CH
COL_ZS = COL_ZA + GDN_V
COL_SMALL = COL_ZS + SSD_D
W_IN_COLS = COL_SMALL + LANES
LANE_G = 0
LANE_DT = GDN_HEADS
LANE_BETA = GDN_HEADS + SSD_HEADS
N_GATE_LANES = GDN_HEADS + SSD_HEADS

MIX_TILE = 256
FFN_TILE = 512
PERM_BLOCK = 256
MIX_CW = 256
FFN_CW = 256
FFN_SUB = 2
GROUP = 4
BLK = 16
RB = CHUNK // SUBLANES
assert RB == SUBLANES


def _bf(x):
    return x.astype(BF16)


def _dot(a, b):
    return jnp.dot(_bf(a), _bf(b), preferred_element_type=F32)


def _dot_nt(a, b):
    return lax.dot_general(_bf(a), _bf(b), (((1,), (1,)), ((), ())), preferred_element_type=F32)


def _rmsnorm(x, w):
    return x * lax.rsqrt(jnp.mean(x * x, axis=-1, keepdims=True) + EPS) * w


def _sigmoid(x):
    return 1.0 / (1.0 + jnp.exp(-x))


def _silu(x):
    return x * _sigmoid(x)


def _softplus(x):
    return jnp.maximum(x, 0.0) + jnp.log1p(jnp.exp(-jnp.abs(x)))


def _row_time(idx):
    return ((idx & (SUBLANES - 1)) << 3) | (idx >> 3)


def _perm_matrix(rows):
    r = np.arange(rows)
    t = (r // CHUNK) * CHUNK + ((r % SUBLANES) * RB + (r % CHUNK) // SUBLANES)
    p = np.zeros((rows, rows), np.float32)
    p[r, t] = 1.0
    return jnp.asarray(p, BF16)


def _to_chunk_order(hb, perm_ref):
    tile = hb.shape[0]
    parts = [jnp.dot(perm_ref[...], hb[r:r + PERM_BLOCK], preferred_element_type=F32).astype(BF16)
             for r in range(0, tile, PERM_BLOCK)]
    return parts[0] if len(parts) == 1 else jnp.concatenate(parts, axis=0)


def _store_natural(o_ref, x_ref, rn, slab_s, row0):
    n = rn.shape[0]
    for lt in range(D_MODEL // LANES):
        slab_s[lt, row0:row0 + n, :] = rn[:, lt * LANES:(lt + 1) * LANES]
    for lt in range(D_MODEL // LANES):
        cols = slice(lt * LANES, (lt + 1) * LANES)
        for c in range(n // CHUNK):
            for s in range(SUBLANES):
                r0 = row0 + c * CHUNK + s * RB
                o_ref[0, r0:r0 + RB, cols] = (x_ref[0, r0:r0 + RB, cols]
                                              + slab_s[lt, pl.ds(row0 + c * CHUNK + s, RB, stride=SUBLANES), :])


def _conv_chunks(streams, carry_ref):
    tile = streams[0][0].shape[0]
    nk = len(streams[0][1])
    sub = lax.broadcasted_iota(jnp.int32, (SUBLANES, streams[0][0].shape[1]), 0)
    prev = [[carry_ref[j * SUBLANES:(j + 1) * SUBLANES, cols] for j in range(nk - 1)] for _, _, _, cols in streams]
    for c in range(tile // CHUNK):
        exts = []
        for si, (u, _, _, _) in enumerate(streams):
            x = [u[c * CHUNK + i * SUBLANES:c * CHUNK + (i + 1) * SUBLANES] for i in range(RB)]
            z = [pltpu.roll(jnp.where(sub == SUBLANES - 1, prev[si][j], x[RB - (nk - 1) + j]), 1, axis=0)
                 for j in range(nk - 1)]
            exts.append(z + x)
            prev[si] = x[RB - (nk - 1):]
        for i in range(RB):
            accs = []
            for (_, taps, bias, _), ext in zip(streams, exts):
                acc = bias + taps[0] * ext[i]
                for k in range(1, nk):
                    acc = acc + taps[k] * ext[i + k]
                accs.append(acc)
            yield c * RB + i, accs
    for (_, _, _, cols), pv in zip(streams, prev):
        for j in range(nk - 1):
            carry_ref[j * SUBLANES:(j + 1) * SUBLANES, cols] = pv[j]


def _chunk_cumsum(x):
    p = [x[0:SUBLANES]]
    for i in range(1, RB):
        p.append(p[-1] + x[i * SUBLANES:(i + 1) * SUBLANES])
    tot = p[-1]
    sub = lax.broadcasted_iota(jnp.int32, tot.shape, 0)
    inc = tot
    s = 1
    while s < SUBLANES:
        inc = inc + jnp.where(sub >= s, pltpu.roll(inc, s, axis=0), 0.0)
        s *= 2
    exc = inc - tot
    return jnp.concatenate([pi + exc for pi in p], axis=0)


def _unit_lower_inverse(a, eye, blk_mask, mm, fill):
    hs = range(len(a))
    d = [jnp.where(blk_mask, a[h], 0.0) for h in hs]
    n = [a[h] - d[h] for h in hs]
    m = [eye - d[h] for h in hs]
    p = [mm(d[h], d[h]) for h in hs]
    fill()
    for _ in range(2):
        m = [m[h] + mm(m[h], p[h]) for h in hs]
        p = [mm(p[h], p[h]) for h in hs]
        fill()
    m = [m[h] + mm(m[h], p[h]) for h in hs]
    fill()
    e = [mm(m[h], n[h]) for h in hs]
    fill()
    f = [eye - e[h] for h in hs]
    e = [mm(e[h], e[h]) for h in hs]
    fill()
    f = [f[h] + mm(f[h], e[h]) for h in hs]
    fill()
    t = [mm(f[h], m[h]) for h in hs]
    fill()
    return t


def _mixer_kernel(x_ref, perm_ref, nw_ref, win_ref, cw_ref, cb_ref, sp_ref, gnw_ref, dsk_ref, snw_ref, wout_ref,
                  pnw_ref, o_ref, h_s, carry_s, act_s, col_s, oy_s, z_s, cat_s, slab_s, gst_s, sst_s):
    tile = x_ref.shape[1]
    n_groups = tile // (CHUNK * GROUP)

    @pl.when(pl.program_id(1) == 0)
    def _():
        carry_s[...] = jnp.zeros(carry_s.shape, F32)
        gst_s[...] = jnp.zeros(gst_s.shape, F32)
        sst_s[...] = jnp.zeros(sst_s.shape, F32)

    h_s[...] = _to_chunk_order(_bf(_rmsnorm(x_ref[0], nw_ref[...])), perm_ref)

    for cbk in range(CONV_CH // MIX_CW):
        cols = slice(cbk * MIX_CW, (cbk + 1) * MIX_CW)
        pre = jnp.dot(h_s[...], win_ref[:, cols], preferred_element_type=F32)
        taps = [jnp.broadcast_to(cw_ref[k:k + 1, cols], (SUBLANES, MIX_CW)) for k in range(CONV_K)]
        bias = jnp.broadcast_to(cb_ref[:, cols], (SUBLANES, MIX_CW))
        if cbk * MIX_CW < 2 * GDN_QK:
            scale = GDN_DK ** -0.5 if cbk * MIX_CW < GDN_QK else 1.0
        else:
            scale = None
        for r, (acc,) in _conv_chunks([(pre, taps, bias, cols)], carry_s):
            a = _silu(acc)
            if scale is not None:
                parts = []
                for hh in range(MIX_CW // GDN_DK):
                    v = a[:, hh * GDN_DK:(hh + 1) * GDN_DK]
                    parts.append(v * (lax.rsqrt(jnp.sum(v * v, axis=-1, keepdims=True) + EPS) * scale))
                a = jnp.concatenate(parts, axis=1)
            act_s[r * SUBLANES:(r + 1) * SUBLANES, cols] = a

    small = jnp.dot(h_s[...], win_ref[:, COL_SMALL:COL_SMALL + LANES], preferred_element_type=F32)
    lane1 = lax.broadcasted_iota(jnp.int32, (1, LANES), 1)
    a_row = jnp.where(lane1 < N_GATE_LANES, -jnp.exp(sp_ref[1:2, :]), 0.0)
    sp = _softplus(small + sp_ref[0:1, :])
    col_s[0] = sp * a_row
    col_s[1] = sp
    col_s[2] = pltpu.roll(_sigmoid(small), LANES - LANE_BETA, axis=1)

    z_s[:, 0:GDN_V] = jnp.dot(h_s[...], win_ref[:, COL_ZA:COL_ZA + GDN_V], preferred_element_type=F32)
    z_s[:, GDN_V:MIX_WIDTH] = jnp.dot(h_s[...], win_ref[:, COL_ZS:COL_ZS + SSD_D], preferred_element_type=F32)

    r_t = _row_time(lax.broadcasted_iota(jnp.int32, (CHUNK, CHUNK), 0))
    c_t = _row_time(lax.broadcasted_iota(jnp.int32, (CHUNK, CHUNK), 1))
    tril = r_t >= c_t
    lane_cl = lax.broadcasted_iota(jnp.int32, (CHUNK, LANES), 1)
    r_t2 = _row_time(lax.broadcasted_iota(jnp.int32, (CHUNK, LANES), 0))
    c_t2 = _row_time(lane_cl & (CHUNK - 1))
    tril2 = r_t2 >= c_t2
    strict2 = r_t2 > c_t2
    blk_shift = BLK.bit_length() - 1
    blk_mask2 = (r_t2 >> blk_shift) == (c_t2 >> blk_shift)
    eye2 = (r_t2 == c_t2).astype(F32)
    first2 = lane_cl < CHUNK
    row_ll = lax.broadcasted_iota(jnp.int32, (LANES, LANES), 0)
    lane_ll = lax.broadcasted_iota(jnp.int32, (LANES, LANES), 1)
    chunk_shift = CHUNK.bit_length() - 1
    diag_blocks = (row_ll >> chunk_shift) == (lane_ll >> chunk_shift)
    zeros_k = jnp.zeros((CHUNK, GDN_DK), BF16)
    zeros_2k = jnp.zeros((CHUNK, 2 * GDN_DK), BF16)

    def pair_mm(x, y):
        yb = _bf(y)
        rhs_bd = jnp.where(diag_blocks, jnp.concatenate([yb, yb], axis=0), 0.0)
        return jnp.dot(_bf(x), rhs_bd, preferred_element_type=F32)

    def pair_rows(t2, l0, l1):
        return jnp.where(lane1 < CHUNK, t2[l0:l0 + 1, :], t2[l1:l1 + 1, :])

    def pair_cols(c, l0, l1):
        return jnp.where(first2, c[:, l0:l0 + 1], c[:, l1:l1 + 1])

    is_g = lane_cl < GDN_HEADS
    neg_inf = -jnp.inf
    n_pairs = SSD_HEADS // 2
    pairs_per_group = n_pairs // SSD_GROUPS
    hs = range(GDN_HEADS)

    def group_body(gi, carry):
        base = gi * (CHUNK * GROUP)
        rows = [pl.ds(pl.multiple_of(base + cc * CHUNK, CHUNK), CHUNK) for cc in range(GROUP)]

        cs, ecs, gl, beta0, cs_t, bd_t, edl_t, edc = [], [], [], [], [], [], [], []
        for cc in range(GROUP):
            gad = col_s[0, rows[cc], :]
            dtv = col_s[1, rows[cc], :]
            b0 = col_s[2, rows[cc], :]
            c_ = _chunk_cumsum(gad)
            last = c_[CHUNK - 1:CHUNK, :]
            e_ = jnp.exp(c_)
            edl = jnp.exp(last - c_)
            cs.append(c_)
            ecs.append(e_)
            gl.append(jnp.exp(last))
            beta0.append(b0)
            cs_t.append(jnp.concatenate([c_, c_], axis=0).T)
            bd_ = jnp.where(is_g, b0, dtv)
            bd_t.append(jnp.concatenate([bd_, bd_], axis=0).T)
            edl_t.append(edl.T)
            edc.append(edl * dtv)

        sst = [sst_s[p] for p in range(n_pairs)]
        prep_jobs, y_jobs, s_jobs = [], [], []
        for cc in range(GROUP):
            cb, b_t, c_g = [], [], []
            ys = [None] * SSD_HEADS
            xp = [None] * n_pairs
            rhs_y = [None] * n_pairs

            def prep_job(cc=cc, cb=cb, b_t=b_t, c_g=c_g, xp=xp):
                for g in range(SSD_GROUPS):
                    bm = act_s[rows[cc], COL_B + g * SSD_STATE:COL_B + (g + 1) * SSD_STATE]
                    cm = act_s[rows[cc], COL_C + g * SSD_STATE:COL_C + (g + 1) * SSD_STATE]
                    cb.append(_dot_nt(cm, bm))
                    b_t.append(_bf(bm.T))
                    c_g.append(cm)
                for p in range(n_pairs):
                    xp[p] = act_s[rows[cc], COL_XS + p * LANES:COL_XS + (p + 1) * LANES]

            def y_job(i, cc=cc, cb=cb, c_g=c_g, xp=xp, ys=ys, rhs_y=rhs_y):
                p, g, ln = i // 2, i // 2 // pairs_per_group, LANE_DT + i
                if i % 2 == 0:
                    rhs_y[p] = jnp.concatenate([_bf(sst[p]), _bf(xp[p])], axis=0)
                lmat = jnp.exp(jnp.where(tril, cs[cc][:, ln:ln + 1] - cs_t[cc][ln:ln + 1, 0:CHUNK], neg_inf))
                mh = cb[g] * lmat * bd_t[cc][ln:ln + 1, 0:CHUNK]
                ce = c_g[g] * ecs[cc][:, ln:ln + 1]
                ys[i] = jnp.dot(jnp.concatenate([_bf(ce), _bf(mh)], axis=1), rhs_y[p], preferred_element_type=F32)

            def s_job(p, cc=cc, b_t=b_t, xp=xp, ys=ys):
                g, l0, l1 = p // pairs_per_group, LANE_DT + 2 * p, LANE_DT + 2 * p + 1
                oy_s[rows[cc], GDN_V + p * LANES:GDN_V + (p + 1) * LANES] = jnp.where(
                    lane_cl < SSD_HEADDIM, ys[2 * p], ys[2 * p + 1])
                coef = jnp.where(lane_cl < SSD_HEADDIM, edc[cc][:, l0:l0 + 1], edc[cc][:, l1:l1 + 1])
                glp = jnp.where(lane1 < SSD_HEADDIM, gl[cc][:, l0:l0 + 1], gl[cc][:, l1:l1 + 1])
                sst[p] = sst[p] * glp + _dot(b_t[g], xp[p] * coef)

            prep_jobs.append(prep_job)
            y_jobs.append([functools.partial(y_job, i) for i in range(SSD_HEADS)])
            s_jobs.append([functools.partial(s_job, p) for p in range(n_pairs)])

        def fill():
            if prep_jobs:
                prep_jobs.pop(0)()

        hp_n = GDN_HEADS // 2
        ch = [(cc, hp) for cc in range(GROUP) for hp in range(hp_n)]
        nch = range(len(ch))

        def head_cols(ref_base, cc, h):
            return act_s[rows[cc], ref_base + h * GDN_DK:ref_base + (h + 1) * GDN_DK]

        q = [[head_cols(0, cc, 2 * hp + e) for e in range(2)] for cc, hp in ch]
        k = [[head_cols(GDN_QK, cc, 2 * hp + e) for e in range(2)] for cc, hp in ch]
        v = [[head_cols(2 * GDN_QK, cc, 2 * hp + e) for e in range(2)] for cc, hp in ch]
        kb = [[_bf(k[i][e]) for e in range(2)] for i in nch]
        kq = []
        for i in nch:
            lhs_kq = jnp.concatenate([jnp.concatenate(kb[i], axis=1),
                                      jnp.concatenate([_bf(q[i][0]), _bf(q[i][1])], axis=1)], axis=0)
            rhs_kq = jnp.concatenate([jnp.concatenate([kb[i][0], zeros_k], axis=1),
                                      jnp.concatenate([zeros_k, kb[i][1]], axis=1)], axis=0)
            kq.append(_dot_nt(lhs_kq, rhs_kq))
        fill()
        dec = [jnp.exp(jnp.where(tril2, pair_cols(cs[cc], 2 * hp, 2 * hp + 1)
                                 - pair_rows(cs_t[cc], 2 * hp, 2 * hp + 1), neg_inf)) for cc, hp in ch]
        a = [jnp.where(strict2, kq[i][0:CHUNK] * pair_cols(beta0[cc], 2 * hp, 2 * hp + 1) * dec[i], 0.0)
             for i, (cc, hp) in enumerate(ch)]
        tinv = _unit_lower_inverse(a, eye2, blk_mask2, pair_mm, fill)
        sol = []
        for i, (cc, hp) in enumerate(ch):
            rhs_h = [jnp.concatenate([_bf(v[i][e]), _bf(k[i][e] * ecs[cc][:, 2 * hp + e:2 * hp + e + 1])], axis=1)
                     for e in range(2)]
            rhs_bd = jnp.concatenate([jnp.concatenate([rhs_h[0], zeros_2k], axis=1),
                                      jnp.concatenate([zeros_2k, rhs_h[1]], axis=1)], axis=0)
            sol.append(_dot(tinv[i] * pair_rows(bd_t[cc], 2 * hp, 2 * hp + 1), rhs_bd))
        fill()
        qkd = [_bf(kq[i][CHUNK:] * dec[i]) for i in nch]

        gst = [gst_s[h] for h in hs]
        sol_w = 2 * GDN_DV
        for cc in range(GROUP):
            wq, u = [], []
            for h in hs:
                i, e = cc * hp_n + h // 2, h % 2
                u.append(sol[i][:, e * sol_w:e * sol_w + GDN_DV])
                wq.append(jnp.concatenate([_bf(sol[i][:, e * sol_w + GDN_DV:(e + 1) * sol_w]),
                                           _bf(q[i][e] * ecs[cc][:, h:h + 1])], axis=0))
            ws = [_dot(wq[h], gst[h]) for h in hs]
            for job in y_jobs[cc]:
                job()
            v_new = [_bf(u[h] - ws[h][0:CHUNK]) for h in hs]
            oqk = []
            for hp in range(hp_n):
                vn_bd = jnp.concatenate([jnp.concatenate([v_new[2 * hp], zeros_k], axis=1),
                                         jnp.concatenate([zeros_k, v_new[2 * hp + 1]], axis=1)], axis=0)
                oqk.append(jnp.dot(qkd[cc * hp_n + hp], vn_bd, preferred_element_type=F32))
            upd = [_dot(k[cc * hp_n + h // 2][h % 2].T * edl_t[cc][h:h + 1, :], v_new[h]) for h in hs]
            for job in s_jobs[cc]:
                job()
            for h in hs:
                oy_s[rows[cc], h * GDN_DV:(h + 1) * GDN_DV] = (
                    ws[h][CHUNK:] + oqk[h // 2][:, (h % 2) * GDN_DV:(h % 2 + 1) * GDN_DV])
                gst[h] = gst[h] * gl[cc][:, h:h + 1] + upd[h]
        for h in hs:
            gst_s[h] = gst[h]
        for p in range(n_pairs):
            sst_s[p] = sst[p]
        return carry

    lax.fori_loop(0, n_groups, group_body, 0)

    for hd in range(GDN_HEADS):
        cols = slice(hd * GDN_DV, (hd + 1) * GDN_DV)
        o = oy_s[:, cols]
        o = o * lax.rsqrt(jnp.mean(o * o, axis=-1, keepdims=True) + EPS) * gnw_ref[:, cols]
        cat_s[:, cols] = _bf(o * _silu(z_s[:, cols]))
    gw = SSD_D // SSD_GROUPS
    for g in range(SSD_GROUPS):
        cols = slice(GDN_V + g * gw, GDN_V + (g + 1) * gw)
        xs = act_s[:, COL_XS + g * gw:COL_XS + (g + 1) * gw]
        y = oy_s[:, cols] + dsk_ref[:, g * gw:(g + 1) * gw] * xs
        y = y * _silu(z_s[:, cols])
        y = y * lax.rsqrt(jnp.mean(y * y, axis=-1, keepdims=True) + EPS) * snw_ref[:, g * gw:(g + 1) * gw]
        cat_s[:, cols] = _bf(y)

    mix = jnp.dot(cat_s[...], wout_ref[...], preferred_element_type=F32)
    _store_natural(o_ref, x_ref, _rmsnorm(mix, pnw_ref[...]), slab_s, 0)


def _ffn_kernel(x_ref, perm_ref, nw_ref, wup_ref, cw_ref, cb_ref, wdn_ref, pnw_ref, o_ref, h_s, carry_s, act_s,
                slab_s):
    tile = x_ref.shape[1]
    sub_rows = tile // FFN_SUB

    @pl.when(pl.program_id(1) == 0)
    def _():
        carry_s[...] = jnp.zeros(carry_s.shape, F32)

    def prenorm(sub):
        rows = slice(sub * sub_rows, (sub + 1) * sub_rows)
        h_s[rows, :] = _to_chunk_order(_bf(_rmsnorm(x_ref[0, rows, :], nw_ref[...])), perm_ref)

    prenorm(0)
    for sub in range(FFN_SUB):
        rows = slice(sub * sub_rows, (sub + 1) * sub_rows)
        for c in range(D_FF // FFN_CW):
            streams = []
            for base in (0, D_FF):
                cols = slice(base + c * FFN_CW, base + (c + 1) * FFN_CW)
                u = jnp.dot(h_s[rows, :], wup_ref[:, cols], preferred_element_type=F32)
                taps = [jnp.broadcast_to(cw_ref[k:k + 1, cols], (SUBLANES, FFN_CW)) for k in range(FFN_CONV_K)]
                streams.append((u, taps, jnp.broadcast_to(cb_ref[:, cols], (SUBLANES, FFN_CW)), cols))
            for r, (g, up) in _conv_chunks(streams, carry_s):
                r0 = sub * sub_rows + r * SUBLANES
                act_s[r0:r0 + SUBLANES, c * FFN_CW:(c + 1) * FFN_CW] = _bf(_silu(g) * up)
        if sub + 1 < FFN_SUB:
            prenorm(sub + 1)
        f = jnp.dot(act_s[rows, :], wdn_ref[...], preferred_element_type=F32)
        _store_natural(o_ref, x_ref, _rmsnorm(f, pnw_ref[...]), slab_s, sub * sub_rows)


def _const_spec(shape):
    return pl.BlockSpec(shape, lambda b, j: (0,) * len(shape), pipeline_mode=pl.Buffered(1))


def _mixer_call(x, nw, win, cw, cb, sp, gnw, dsk, snw, wout, pnw):
    bsz, seq, _ = x.shape
    tile = MIX_TILE
    assert seq % tile == 0 and tile % (CHUNK * GROUP) == 0 and tile % PERM_BLOCK == 0
    tok_spec = pl.BlockSpec((1, tile, D_MODEL), lambda b, j: (b, j, 0))
    consts = (_perm_matrix(PERM_BLOCK), nw, win, cw, cb, sp, gnw, dsk, snw, wout, pnw)
    return pl.pallas_call(
        _mixer_kernel,
        grid=(bsz, seq // tile),
        in_specs=[tok_spec] + [_const_spec(c.shape) for c in consts],
        out_specs=tok_spec,
        out_shape=jax.ShapeDtypeStruct(x.shape, x.dtype),
        scratch_shapes=[
            pltpu.VMEM((tile, D_MODEL), BF16),
            pltpu.VMEM(((CONV_K - 1) * SUBLANES, CONV_CH), F32),
            pltpu.VMEM((tile, CONV_CH), F32),
            pltpu.VMEM((3, tile, LANES), F32),
            pltpu.VMEM((tile, MIX_WIDTH), F32),
            pltpu.VMEM((tile, MIX_WIDTH), F32),
            pltpu.VMEM((tile, MIX_WIDTH), BF16),
            pltpu.VMEM((D_MODEL // LANES, tile, LANES), F32),
            pltpu.VMEM((GDN_HEADS, GDN_DK, GDN_DV), F32),
            pltpu.VMEM((SSD_HEADS // 2, SSD_STATE, LANES), F32),
        ],
        compiler_params=pltpu.CompilerParams(
            dimension_semantics=("arbitrary", "arbitrary"), vmem_limit_bytes=VMEM_LIMIT_BYTES),
        name="mixer",
    )(x, *consts)


def _ffn_call(x, nw, wup, cw, cb, wdn, pnw):
    bsz, seq, _ = x.shape
    tile = FFN_TILE
    assert seq % tile == 0 and tile % (FFN_SUB * PERM_BLOCK) == 0
    tok_spec = pl.BlockSpec((1, tile, D_MODEL), lambda b, j: (b, j, 0))
    consts = (_perm_matrix(PERM_BLOCK), nw, wup, cw, cb, wdn, pnw)
    return pl.pallas_call(
        _ffn_kernel,
        grid=(bsz, seq // tile),
        in_specs=[tok_spec] + [_const_spec(c.shape) for c in consts],
        out_specs=tok_spec,
        out_shape=jax.ShapeDtypeStruct(x.shape, x.dtype),
        scratch_shapes=[
            pltpu.VMEM((tile, D_MODEL), BF16),
            pltpu.VMEM(((FFN_CONV_K - 1) * SUBLANES, 2 * D_FF), F32),
            pltpu.VMEM((tile, D_FF), BF16),
            pltpu.VMEM((D_MODEL // LANES, tile, LANES), F32),
        ],
        compiler_params=pltpu.CompilerParams(
            dimension_semantics=("arbitrary", "arbitrary"), vmem_limit_bytes=VMEM_LIMIT_BYTES),
        name="ffn",
    )(x, *consts)


def _row(v):
    return v.reshape(1, -1).astype(F32)


def _layer(x, pre_mix_norm, w_in, gdn_conv_w, gdn_a_log, gdn_dt_bias, gdn_norm_w, ssd_conv_w, ssd_conv_b,
           ssd_a_log, ssd_dt_bias, ssd_d, ssd_norm_w, w_out, post_mix_norm, pre_ffn_norm, w_up, ffn_conv_w,
           ffn_conv_b, w_down, post_ffn_norm):
    offs = [0]
    for n in IN_SPLITS:
        offs.append(offs[-1] + n)
    seg = [w_in[:, offs[i]:offs[i + 1]] for i in range(len(IN_SPLITS))]
    wq, wk, wv, wza, wb, wa, wzs, wxs, wbs, wcs, wdt = seg
    pad = jnp.zeros((D_MODEL, LANES - N_GATE_LANES - GDN_HEADS), w_in.dtype)
    win = _bf(jnp.concatenate([wq, wk, wv, wxs, wbs, wcs, wza, wzs, wa, wdt, wb, pad], axis=1))
    cw = jnp.concatenate([gdn_conv_w, ssd_conv_w], axis=1).astype(F32)
    cb = jnp.concatenate([jnp.zeros((GDN_CONV_CH,), F32), ssd_conv_b.astype(F32)]).reshape(1, CONV_CH)
    zpad = jnp.zeros((LANES - N_GATE_LANES,), F32)
    sp = jnp.zeros((SUBLANES, LANES), F32)
    sp = sp.at[0].set(jnp.concatenate([gdn_dt_bias.astype(F32), ssd_dt_bias.astype(F32), zpad]))
    sp = sp.at[1].set(jnp.concatenate([gdn_a_log.astype(F32), ssd_a_log.astype(F32), zpad]))
    gnw = _row(jnp.tile(gdn_norm_w, GDN_HEADS))
    dsk = _row(jnp.repeat(ssd_d, SSD_HEADDIM))
    x1 = _mixer_call(x, _row(pre_mix_norm), win, cw, cb, sp, gnw, dsk, _row(ssd_norm_w), _bf(w_out),
                     _row(post_mix_norm))
    return _ffn_call(x1, _row(pre_ffn_norm), _bf(w_up), ffn_conv_w.astype(F32), _row(ffn_conv_b), _bf(w_down),
                     _row(post_ffn_norm))


def kernel(x, pre_mix_norm, w_in, gdn_conv_w, gdn_a_log, gdn_dt_bias, gdn_norm_w, ssd_conv_w, ssd_conv_b, ssd_a_log, ssd_dt_bias, ssd_d, ssd_norm_w, w_out, post_mix_norm, pre_ffn_norm, w_up, ffn_conv_w, ffn_conv_b, w_down, post_ffn_norm):
    params = (pre_mix_norm, w_in, gdn_conv_w, gdn_a_log, gdn_dt_bias, gdn_norm_w, ssd_conv_w, ssd_conv_b, ssd_a_log,
              ssd_dt_bias, ssd_d, ssd_norm_w, w_out, post_mix_norm, pre_ffn_norm, w_up, ffn_conv_w, ffn_conv_b,
              w_down, post_ffn_norm)
    for layer in range(pre_mix_norm.shape[0]):
        x = _layer(x, *(p[layer] for p in params))
    return x
```

```python
import functools

import jax
import jax.numpy as jnp
import numpy as np
from jax import lax
from jax.experimental import pallas as pl
from jax.experimental.pallas import tpu as pltpu

F32 = jnp.float32
BF16 = jnp.bfloat16

D_MODEL = 1024
GDN_HEADS = 8
GDN_DK = 128
GDN_DV = 128
SSD_HEADS = 16
SSD_HEADDIM = 64
SSD_GROUPS = 2
SSD_STATE = 128
CONV_K = 4
CHUNK = 64
D_FF = 2816
FFN_CONV_K = 3
EPS = 1e-6

GDN_QK = GDN_HEADS * GDN_DK
GDN_V = GDN_HEADS * GDN_DV
SSD_D = SSD_HEADS * SSD_HEADDIM
SSD_BC = SSD_GROUPS * SSD_STATE
MIX_WIDTH = GDN_V + SSD_D
GDN_CONV_CH = 2 * GDN_QK + GDN_V
SSD_CONV_CH = SSD_D + 2 * SSD_BC
CONV_CH = GDN_CONV_CH + SSD_CONV_CH
IN_SPLITS = (GDN_QK, GDN_QK, GDN_V, GDN_V, GDN_HEADS, GDN_HEADS, SSD_D, SSD_D, SSD_BC, SSD_BC, SSD_HEADS)

LANES = 128
SUBLANES = 8
VMEM_LIMIT_BYTES = 56 * 1024 * 1024

COL_XS = GDN_CONV_CH
COL_B = COL_XS + SSD_D
COL_C = COL_B + SSD_BC
COL_ZA = CONV_CH
COL_ZS = COL_ZA + GDN_V
COL_SMALL = COL_ZS + SSD_D
W_IN_COLS = COL_SMALL + LANES
LANE_G = 0
LANE_DT = GDN_HEADS
LANE_BETA = GDN_HEADS + SSD_HEADS
N_GATE_LANES = GDN_HEADS + SSD_HEADS

MIX_TILE = 256
FFN_TILE = 512
PERM_BLOCK = 256
MIX_CW = 256
FFN_CW = 256
FFN_SUB = 2
GROUP = 4
RB = CHUNK // SUBLANES
assert RB == SUBLANES


def _bf(x):
    return x.astype(BF16)


def _dot(a, b):
    return jnp.dot(_bf(a), _bf(b), preferred_element_type=F32)


def _dot_nt(a, b):
    return lax.dot_general(_bf(a), _bf(b), (((1,), (1,)), ((), ())), preferred_element_type=F32)


def _rmsnorm(x, w):
    return x * lax.rsqrt(jnp.mean(x * x, axis=-1, keepdims=True) + EPS) * w


def _sigmoid(x):
    return 1.0 / (1.0 + jnp.exp(-x))


def _silu(x):
    return x * _sigmoid(x)


def _softplus(x):
    return jnp.maximum(x, 0.0) + jnp.log1p(jnp.exp(-jnp.abs(x)))


def _row_time(idx):
    return ((idx & (SUBLANES - 1)) << 3) | (idx >> 3)


def _perm_matrix(rows):
    r = np.arange(rows)
    t = (r // CHUNK) * CHUNK + ((r % SUBLANES) * RB + (r % CHUNK) // SUBLANES)
    p = np.zeros((rows, rows), np.float32)
    p[r, t] = 1.0
    return jnp.asarray(p, BF16)


def _to_chunk_order(hb, perm_ref):
    tile = hb.shape[0]
    parts = [jnp.dot(perm_ref[...], hb[r:r + PERM_BLOCK], preferred_element_type=F32).astype(BF16)
             for r in range(0, tile, PERM_BLOCK)]
    return parts[0] if len(parts) == 1 else jnp.concatenate(parts, axis=0)


def _store_natural(o_ref, x_ref, rn, slab_s, row0):
    n = rn.shape[0]
    for lt in range(D_MODEL // LANES):
        slab_s[lt, row0:row0 + n, :] = rn[:, lt * LANES:(lt + 1) * LANES]
    for lt in range(D_MODEL // LANES):
        cols = slice(lt * LANES, (lt + 1) * LANES)
        for c in range(n // CHUNK):
            for s in range(SUBLANES):
                r0 = row0 + c * CHUNK + s * RB
                o_ref[0, r0:r0 + RB, cols] = (x_ref[0, r0:r0 + RB, cols]
                                              + slab_s[lt, pl.ds(row0 + c * CHUNK + s, RB, stride=SUBLANES), :])


def _conv_chunks(streams, carry_ref):
    tile = streams[0][0].shape[0]
    nk = len(streams[0][1])
    sub = lax.broadcasted_iota(jnp.int32, (SUBLANES, streams[0][0].shape[1]), 0)
    prev = [[carry_ref[j * SUBLANES:(j + 1) * SUBLANES, cols] for j in range(nk - 1)] for _, _, _, cols in streams]
    for c in range(tile // CHUNK):
        exts = []
        for si, (u, _, _, _) in enumerate(streams):
            x = [u[c * CHUNK + i * SUBLANES:c * CHUNK + (i + 1) * SUBLANES] for i in range(RB)]
            z = [pltpu.roll(jnp.where(sub == SUBLANES - 1, prev[si][j], x[RB - (nk - 1) + j]), 1, axis=0)
                 for j in range(nk - 1)]
            exts.append(z + x)
            prev[si] = x[RB - (nk - 1):]
        for i in range(RB):
            accs = []
            for (_, taps, bias, _), ext in zip(streams, exts):
                acc = bias + taps[0] * ext[i]
                for k in range(1, nk):
                    acc = acc + taps[k] * ext[i + k]
                accs.append(acc)
            yield c * RB + i, accs
    for (_, _, _, cols), pv in zip(streams, prev):
        for j in range(nk - 1):
            carry_ref[j * SUBLANES:(j + 1) * SUBLANES, cols] = pv[j]


def _chunk_cumsum(x):
    p = [x[0:SUBLANES]]
    for i in range(1, RB):
        p.append(p[-1] + x[i * SUBLANES:(i + 1) * SUBLANES])
    tot = p[-1]
    sub = lax.broadcasted_iota(jnp.int32, tot.shape, 0)
    inc = tot
    s = 1
    while s < SUBLANES:
        inc = inc + jnp.where(sub >= s, pltpu.roll(inc, s, axis=0), 0.0)
        s *= 2
    exc = inc - tot
    return jnp.concatenate([pi + exc for pi in p], axis=0)


def _unit_lower_inverse(a, eye, time_r, time_c, mm, fill):
    hs = range(len(a))

    def coupling(level):
        return ((time_r >> (level + 1)) == (time_c >> (level + 1))) & ((time_r >> level) != (time_c >> level))

    t = [eye - jnp.where(coupling(0), a[h], 0.0) for h in hs]
    for level in range(1, CHUNK.bit_length() - 1):
        mask = coupling(level)
        x = [mm(jnp.where(mask, a[h], 0.0), t[h]) for h in hs]
        fill()
        t = [t[h] - mm(t[h], x[h]) for h in hs]
        fill()
    return t


def _mixer_kernel(x_ref, perm_ref, nw_ref, win_ref, cw_ref, cb_ref, sp_ref, gnw_ref, dsk_ref, snw_ref, wout_ref,
                  pnw_ref, o_ref, h_s, carry_s, act_s, col_s, oy_s, z_s, cat_s, slab_s, gst_s, sst_s):
    tile = x_ref.shape[1]
    n_groups = tile // (CHUNK * GROUP)

    @pl.when(pl.program_id(1) == 0)
    def _():
        carry_s[...] = jnp.zeros(carry_s.shape, F32)
        gst_s[...] = jnp.zeros(gst_s.shape, F32)
        sst_s[...] = jnp.zeros(sst_s.shape, F32)

    h_s[...] = _to_chunk_order(_bf(_rmsnorm(x_ref[0], nw_ref[...])), perm_ref)

    for cbk in range(CONV_CH // MIX_CW):
        cols = slice(cbk * MIX_CW, (cbk + 1) * MIX_CW)
        pre = jnp.dot(h_s[...], win_ref[:, cols], preferred_element_type=F32)
        taps = [jnp.broadcast_to(cw_ref[k:k + 1, cols], (SUBLANES, MIX_CW)) for k in range(CONV_K)]
        bias = jnp.broadcast_to(cb_ref[:, cols], (SUBLANES, MIX_CW))
        if cbk * MIX_CW < 2 * GDN_QK:
            scale = GDN_DK ** -0.5 if cbk * MIX_CW < GDN_QK else 1.0
        else:
            scale = None
        for r, (acc,) in _conv_chunks([(pre, taps, bias, cols)], carry_s):
            a = _silu(acc)
            if scale is not None:
                parts = []
                for hh in range(MIX_CW // GDN_DK):
                    v = a[:, hh * GDN_DK:(hh + 1) * GDN_DK]
                    parts.append(v * (lax.rsqrt(jnp.sum(v * v, axis=-1, keepdims=True) + EPS) * scale))
                a = jnp.concatenate(parts, axis=1)
            act_s[r * SUBLANES:(r + 1) * SUBLANES, cols] = a

    small = jnp.dot(h_s[...], win_ref[:, COL_SMALL:COL_SMALL + LANES], preferred_element_type=F32)
    lane1 = lax.broadcasted_iota(jnp.int32, (1, LANES), 1)
    a_row = jnp.where(lane1 < N_GATE_LANES, -jnp.exp(sp_ref[1:2, :]), 0.0)
    sp = _softplus(small + sp_ref[0:1, :])
    col_s[0] = sp * a_row
    col_s[1] = sp
    col_s[2] = pltpu.roll(_sigmoid(small), LANES - LANE_BETA, axis=1)

    z_s[:, 0:GDN_V] = jnp.dot(h_s[...], win_ref[:, COL_ZA:COL_ZA + GDN_V], preferred_element_type=F32)
    z_s[:, GDN_V:MIX_WIDTH] = jnp.dot(h_s[...], win_ref[:, COL_ZS:COL_ZS + SSD_D], preferred_element_type=F32)

    r_t = _row_time(lax.broadcasted_iota(jnp.int32, (CHUNK, CHUNK), 0))
    c_t = _row_time(lax.broadcasted_iota(jnp.int32, (CHUNK, CHUNK), 1))
    tril = r_t >= c_t
    lane_cl = lax.broadcasted_iota(jnp.int32, (CHUNK, LANES), 1)
    r_t2 = _row_time(lax.broadcasted_iota(jnp.int32, (CHUNK, LANES), 0))
    c_t2 = _row_time(lane_cl & (CHUNK - 1))
    tril2 = r_t2 >= c_t2
    strict2 = r_t2 > c_t2
    eye2 = (r_t2 == c_t2).astype(F32)
    first2 = lane_cl < CHUNK
    row_ll = lax.broadcasted_iota(jnp.int32, (LANES, LANES), 0)
    lane_ll = lax.broadcasted_iota(jnp.int32, (LANES, LANES), 1)
    chunk_shift = CHUNK.bit_length() - 1
    diag_blocks = (row_ll >> chunk_shift) == (lane_ll >> chunk_shift)
    zeros_k = jnp.zeros((CHUNK, GDN_DK), BF16)
    zeros_2k = jnp.zeros((CHUNK, 2 * GDN_DK), BF16)

    def pair_mm(x, y):
        yb = _bf(y)
        rhs_bd = jnp.where(diag_blocks, jnp.concatenate([yb, yb], axis=0), 0.0)
        return jnp.dot(_bf(x), rhs_bd, preferred_element_type=F32)

    def pair_rows(t2, l0, l1):
        return jnp.where(lane1 < CHUNK, t2[l0:l0 + 1, :], t2[l1:l1 + 1, :])

    def pair_cols(c, l0, l1):
        return jnp.where(first2, c[:, l0:l0 + 1], c[:, l1:l1 + 1])

    is_g = lane_cl < GDN_HEADS
    neg_inf = -jnp.inf
    n_pairs = SSD_HEADS // 2
    pairs_per_group = n_pairs // SSD_GROUPS
    hs = range(GDN_HEADS)

    def group_body(gi, carry):
        base = gi * (CHUNK * GROUP)
        rows = [pl.ds(pl.multiple_of(base + cc * CHUNK, CHUNK), CHUNK) for cc in range(GROUP)]

        cs, ecs, gl, beta0, cs_t, bd_t, edl_t, edc = [], [], [], [], [], [], [], []
        for cc in range(GROUP):
            gad = col_s[0, rows[cc], :]
            dtv = col_s[1, rows[cc], :]
            b0 = col_s[2, rows[cc], :]
            c_ = _chunk_cumsum(gad)
            last = c_[CHUNK - 1:CHUNK, :]
            e_ = jnp.exp(c_)
            edl = jnp.exp(last - c_)
            cs.append(c_)
            ecs.append(e_)
            gl.append(jnp.exp(last))
            beta0.append(b0)
            cs_t.append(jnp.concatenate([c_, c_], axis=0).T)
            bd_ = jnp.where(is_g, b0, dtv)
            bd_t.append(jnp.concatenate([bd_, bd_], axis=0).T)
            edl_t.append(edl.T)
            edc.append(edl * dtv)

        sst = [sst_s[p] for p in range(n_pairs)]
        prep_jobs, y_jobs, s_jobs = [], [], []
        for cc in range(GROUP):
            cb, b_t, c_g = [], [], []
            ys = [None] * SSD_HEADS
            xp = [None] * n_pairs
            rhs_y = [None] * n_pairs

            def prep_job(cc=cc, cb=cb, b_t=b_t, c_g=c_g, xp=xp):
                for g in range(SSD_GROUPS):
                    bm = act_s[rows[cc], COL_B + g * SSD_STATE:COL_B + (g + 1) * SSD_STATE]
                    cm = act_s[rows[cc], COL_C + g * SSD_STATE:COL_C + (g + 1) * SSD_STATE]
                    cb.append(_dot_nt(cm, bm))
                    b_t.append(_bf(bm.T))
                    c_g.append(cm)
                for p in range(n_pairs):
                    xp[p] = act_s[rows[cc], COL_XS + p * LANES:COL_XS + (p + 1) * LANES]

            def y_job(i, cc=cc, cb=cb, c_g=c_g, xp=xp, ys=ys, rhs_y=rhs_y):
                p, g, ln = i // 2, i // 2 // pairs_per_group, LANE_DT + i
                if i % 2 == 0:
                    rhs_y[p] = jnp.concatenate([_bf(sst[p]), _bf(xp[p])], axis=0)
                lmat = jnp.exp(jnp.where(tril, cs[cc][:, ln:ln + 1] - cs_t[cc][ln:ln + 1, 0:CHUNK], neg_inf))
                mh = cb[g] * lmat * bd_t[cc][ln:ln + 1, 0:CHUNK]
                ce = c_g[g] * ecs[cc][:, ln:ln + 1]
                ys[i] = jnp.dot(jnp.concatenate([_bf(ce), _bf(mh)], axis=1), rhs_y[p], preferred_element_type=F32)

            def s_job(p, cc=cc, b_t=b_t, xp=xp, ys=ys):
                g, l0, l1 = p // pairs_per_group, LANE_DT + 2 * p, LANE_DT + 2 * p + 1
                oy_s[rows[cc], GDN_V + p * LANES:GDN_V + (p + 1) * LANES] = jnp.where(
                    lane_cl < SSD_HEADDIM, ys[2 * p], ys[2 * p + 1])
                coef = jnp.where(lane_cl < SSD_HEADDIM, edc[cc][:, l0:l0 + 1], edc[cc][:, l1:l1 + 1])
                glp = jnp.where(lane1 < SSD_HEADDIM, gl[cc][:, l0:l0 + 1], gl[cc][:, l1:l1 + 1])
                sst[p] = sst[p] * glp + _dot(b_t[g], xp[p] * coef)

            prep_jobs.append(prep_job)
            y_jobs.append([functools.partial(y_job, i) for i in range(SSD_HEADS)])
            s_jobs.append([functools.partial(s_job, p) for p in range(n_pairs)])

        def fill():
            if prep_jobs:
                prep_jobs.pop(0)()

        hp_n = GDN_HEADS // 2
        ch = [(cc, hp) for cc in range(GROUP) for hp in range(hp_n)]
        nch = range(len(ch))

        def head_cols(ref_base, cc, h):
            return act_s[rows[cc], ref_base + h * GDN_DK:ref_base + (h + 1) * GDN_DK]

        q = [[head_cols(0, cc, 2 * hp + e) for e in range(2)] for cc, hp in ch]
        k = [[head_cols(GDN_QK, cc, 2 * hp + e) for e in range(2)] for cc, hp in ch]
        v = [[head_cols(2 * GDN_QK, cc, 2 * hp + e) for e in range(2)] for cc, hp in ch]
        kb = [[_bf(k[i][e]) for e in range(2)] for i in nch]
        kq = []
        for i in nch:
            lhs_kq = jnp.concatenate([jnp.concatenate(kb[i], axis=1),
                                      jnp.concatenate([_bf(q[i][0]), _bf(q[i][1])], axis=1)], axis=0)
            rhs_kq = jnp.concatenate([jnp.concatenate([kb[i][0], zeros_k], axis=1),
                                      jnp.concatenate([zeros_k, kb[i][1]], axis=1)], axis=0)
            kq.append(_dot_nt(lhs_kq, rhs_kq))
        fill()
        dec = [jnp.exp(jnp.where(tril2, pair_cols(cs[cc], 2 * hp, 2 * hp + 1)
                                 - pair_rows(cs_t[cc], 2 * hp, 2 * hp + 1), neg_inf)) for cc, hp in ch]
        a = [jnp.where(strict2, kq[i][0:CHUNK] * pair_cols(beta0[cc], 2 * hp, 2 * hp + 1) * dec[i], 0.0)
             for i, (cc, hp) in enumerate(ch)]
        tinv = _unit_lower_inverse(a, eye2, r_t2, c_t2, pair_mm, fill)
        sol = []
        for i, (cc, hp) in enumerate(ch):
            rhs_h = [jnp.concatenate([_bf(v[i][e]), _bf(k[i][e] * ecs[cc][:, 2 * hp + e:2 * hp + e + 1])], axis=1)
                     for e in range(2)]
            rhs_bd = jnp.concatenate([jnp.concatenate([rhs_h[0], zeros_2k], axis=1),
                                      jnp.concatenate([zeros_2k, rhs_h[1]], axis=1)], axis=0)
            sol.append(_dot(tinv[i] * pair_rows(bd_t[cc], 2 * hp, 2 * hp + 1), rhs_bd))
        fill()
        qkd = [_bf(kq[i][CHUNK:] * dec[i]) for i in nch]

        gst = [gst_s[h] for h in hs]
        sol_w = 2 * GDN_DV
        for cc in range(GROUP):
            wq, u = [], []
            for h in hs:
                i, e = cc * hp_n + h // 2, h % 2
                u.append(sol[i][:, e * sol_w:e * sol_w + GDN_DV])
                wq.append(jnp.concatenate([_bf(sol[i][:, e * sol_w + GDN_DV:(e + 1) * sol_w]),
                                           _bf(q[i][e] * ecs[cc][:, h:h + 1])], axis=0))
            ws = [_dot(wq[h], gst[h]) for h in hs]
            for job in y_jobs[cc]:
                job()
            v_new = [_bf(u[h] - ws[h][0:CHUNK]) for h in hs]
            oqk = []
            for hp in range(hp_n):
                vn_bd = jnp.concatenate([jnp.concatenate([v_new[2 * hp], zeros_k], axis=1),
                                         jnp.concatenate([zeros_k, v_new[2 * hp + 1]], axis=1)], axis=0)
                oqk.append(jnp.dot(qkd[cc * hp_n + hp], vn_bd, preferred_element_type=F32))
            upd = [_dot(k[cc * hp_n + h // 2][h % 2].T * edl_t[cc][h:h + 1, :], v_new[h]) for h in hs]
            for job in s_jobs[cc]:
                job()
            for h in hs:
                oy_s[rows[cc], h * GDN_DV:(h + 1) * GDN_DV] = (
                    ws[h][CHUNK:] + oqk[h // 2][:, (h % 2) * GDN_DV:(h % 2 + 1) * GDN_DV])
                gst[h] = gst[h] * gl[cc][:, h:h + 1] + upd[h]
        for h in hs:
            gst_s[h] = gst[h]
        for p in range(n_pairs):
            sst_s[p] = sst[p]
        return carry

    lax.fori_loop(0, n_groups, group_body, 0)

    for hd in range(GDN_HEADS):
        cols = slice(hd * GDN_DV, (hd + 1) * GDN_DV)
        o = oy_s[:, cols]
        o = o * lax.rsqrt(jnp.mean(o * o, axis=-1, keepdims=True) + EPS) * gnw_ref[:, cols]
        cat_s[:, cols] = _bf(o * _silu(z_s[:, cols]))
    gw = SSD_D // SSD_GROUPS
    for g in range(SSD_GROUPS):
        cols = slice(GDN_V + g * gw, GDN_V + (g + 1) * gw)
        xs = act_s[:, COL_XS + g * gw:COL_XS + (g + 1) * gw]
        y = oy_s[:, cols] + dsk_ref[:, g * gw:(g + 1) * gw] * xs
        y = y * _silu(z_s[:, cols])
        y = y * lax.rsqrt(jnp.mean(y * y, axis=-1, keepdims=True) + EPS) * snw_ref[:, g * gw:(g + 1) * gw]
        cat_s[:, cols] = _bf(y)

    mix = jnp.dot(cat_s[...], wout_ref[...], preferred_element_type=F32)
    _store_natural(o_ref, x_ref, _rmsnorm(mix, pnw_ref[...]), slab_s, 0)


def _ffn_kernel(x_ref, perm_ref, nw_ref, wup_ref, cw_ref, cb_ref, wdn_ref, pnw_ref, o_ref, h_s, carry_s, act_s,
                slab_s):
    tile = x_ref.shape[1]
    sub_rows = tile // FFN_SUB

    @pl.when(pl.program_id(1) == 0)
    def _():
        carry_s[...] = jnp.zeros(carry_s.shape, F32)

    def prenorm(sub):
        rows = slice(sub * sub_rows, (sub + 1) * sub_rows)
        h_s[rows, :] = _to_chunk_order(_bf(_rmsnorm(x_ref[0, rows, :], nw_ref[...])), perm_ref)

    prenorm(0)
    for sub in range(FFN_SUB):
        rows = slice(sub * sub_rows, (sub + 1) * sub_rows)
        for c in range(D_FF // FFN_CW):
            streams = []
            for base in (0, D_FF):
                cols = slice(base + c * FFN_CW, base + (c + 1) * FFN_CW)
                u = jnp.dot(h_s[rows, :], wup_ref[:, cols], preferred_element_type=F32)
                taps = [jnp.broadcast_to(cw_ref[k:k + 1, cols], (SUBLANES, FFN_CW)) for k in range(FFN_CONV_K)]
                streams.append((u, taps, jnp.broadcast_to(cb_ref[:, cols], (SUBLANES, FFN_CW)), cols))
            for r, (g, up) in _conv_chunks(streams, carry_s):
                r0 = sub * sub_rows + r * SUBLANES
                act_s[r0:r0 + SUBLANES, c * FFN_CW:(c + 1) * FFN_CW] = _bf(_silu(g) * up)
        if sub + 1 < FFN_SUB:
            prenorm(sub + 1)
        f = jnp.dot(act_s[rows, :], wdn_ref[...], preferred_element_type=F32)
        _store_natural(o_ref, x_ref, _rmsnorm(f, pnw_ref[...]), slab_s, sub * sub_rows)


def _const_spec(shape):
    return pl.BlockSpec(shape, lambda b, j: (0,) * len(shape), pipeline_mode=pl.Buffered(1))


def _mixer_call(x, nw, win, cw, cb, sp, gnw, dsk, snw, wout, pnw):
    bsz, seq, _ = x.shape
    tile = MIX_TILE
    assert seq % tile == 0 and tile % (CHUNK * GROUP) == 0 and tile % PERM_BLOCK == 0
    tok_spec = pl.BlockSpec((1, tile, D_MODEL), lambda b, j: (b, j, 0))
    consts = (_perm_matrix(PERM_BLOCK), nw, win, cw, cb, sp, gnw, dsk, snw, wout, pnw)
    return pl.pallas_call(
        _mixer_kernel,
        grid=(bsz, seq // tile),
        in_specs=[tok_spec] + [_const_spec(c.shape) for c in consts],
        out_specs=tok_spec,
        out_shape=jax.ShapeDtypeStruct(x.shape, x.dtype),
        scratch_shapes=[
            pltpu.VMEM((tile, D_MODEL), BF16),
            pltpu.VMEM(((CONV_K - 1) * SUBLANES, CONV_CH), F32),
            pltpu.VMEM((tile, CONV_CH), F32),
            pltpu.VMEM((3, tile, LANES), F32),
            pltpu.VMEM((tile, MIX_WIDTH), F32),
            pltpu.VMEM((tile, MIX_WIDTH), F32),
            pltpu.VMEM((tile, MIX_WIDTH), BF16),
            pltpu.VMEM((D_MODEL // LANES, tile, LANES), F32),
            pltpu.VMEM((GDN_HEADS, GDN_DK, GDN_DV), F32),
            pltpu.VMEM((SSD_HEADS // 2, SSD_STATE, LANES), F32),
        ],
        compiler_params=pltpu.CompilerParams(
            dimension_semantics=("arbitrary", "arbitrary"), vmem_limit_bytes=VMEM_LIMIT_BYTES),
        name="mixer",
    )(x, *consts)


def _ffn_call(x, nw, wup, cw, cb, wdn, pnw):
    bsz, seq, _ = x.shape
    tile = FFN_TILE
    assert seq % tile == 0 and tile % (FFN_SUB * PERM_BLOCK) == 0
    tok_spec = pl.BlockSpec((1, tile, D_MODEL), lambda b, j: (b, j, 0))
    consts = (_perm_matrix(PERM_BLOCK), nw, wup, cw, cb, wdn, pnw)
    return pl.pallas_call(
        _ffn_kernel,
        grid=(bsz, seq // tile),
        in_specs=[tok_spec] + [_const_spec(c.shape) for c in consts],
        out_specs=tok_spec,
        out_shape=jax.ShapeDtypeStruct(x.shape, x.dtype),
        scratch_shapes=[
            pltpu.VMEM((tile, D_MODEL), BF16),
            pltpu.VMEM(((FFN_CONV_K - 1) * SUBLANES, 2 * D_FF), F32),
            pltpu.VMEM((tile, D_FF), BF16),
            pltpu.VMEM((D_MODEL // LANES, tile, LANES), F32),
        ],
        compiler_params=pltpu.CompilerParams(
            dimension_semantics=("arbitrary", "arbitrary"), vmem_limit_bytes=VMEM_LIMIT_BYTES),
        name="ffn",
    )(x, *consts)


def _row(v):
    return v.reshape(1, -1).astype(F32)


def _layer(x, pre_mix_norm, w_in, gdn_conv_w, gdn_a_log, gdn_dt_bias, gdn_norm_w, ssd_conv_w, ssd_conv_b,
           ssd_a_log, ssd_dt_bias, ssd_d, ssd_norm_w, w_out, post_mix_norm, pre_ffn_norm, w_up, ffn_conv_w,
           ffn_conv_b, w_down, post_ffn_norm):
    offs = [0]
    for n in IN_SPLITS:
        offs.append(offs[-1] + n)
    seg = [w_in[:, offs[i]:offs[i + 1]] for i in range(len(IN_SPLITS))]
    wq, wk, wv, wza, wb, wa, wzs, wxs, wbs, wcs, wdt = seg
    pad = jnp.zeros((D_MODEL, LANES - N_GATE_LANES - GDN_HEADS), w_in.dtype)
    win = _bf(jnp.concatenate([wq, wk, wv, wxs, wbs, wcs, wza, wzs, wa, wdt, wb, pad], axis=1))
    cw = jnp.concatenate([gdn_conv_w, ssd_conv_w], axis=1).astype(F32)
    cb = jnp.concatenate([jnp.zeros((GDN_CONV_CH,), F32), ssd_conv_b.astype(F32)]).reshape(1, CONV_CH)
    zpad = jnp.zeros((LANES - N_GATE_LANES,), F32)
    sp = jnp.zeros((SUBLANES, LANES), F32)
    sp = sp.at[0].set(jnp.concatenate([gdn_dt_bias.astype(F32), ssd_dt_bias.astype(F32), zpad]))
    sp = sp.at[1].set(jnp.concatenate([gdn_a_log.astype(F32), ssd_a_log.astype(F32), zpad]))
    gnw = _row(jnp.tile(gdn_norm_w, GDN_HEADS))
    dsk = _row(jnp.repeat(ssd_d, SSD_HEADDIM))
    x1 = _mixer_call(x, _row(pre_mix_norm), win, cw, cb, sp, gnw, dsk, _row(ssd_norm_w), _bf(w_out),
                     _row(post_mix_norm))
    return _ffn_call(x1, _row(pre_ffn_norm), _bf(w_up), ffn_conv_w.astype(F32), _row(ffn_conv_b), _bf(w_down),
                     _row(post_ffn_norm))


def kernel(x, pre_mix_norm, w_in, gdn_conv_w, gdn_a_log, gdn_dt_bias, gdn_norm_w, ssd_conv_w, ssd_conv_b, ssd_a_log, ssd_dt_bias, ssd_d, ssd_norm_w, w_out, post_mix_norm, pre_ffn_norm, w_up, ffn_conv_w, ffn_conv_b, w_down, post_ffn_norm):
    params = (pre_mix_norm, w_in, gdn_conv_w, gdn_a_log, gdn_dt_bias, gdn_norm_w, ssd_conv_w, ssd_conv_b, ssd_a_log,
              ssd_dt_bias, ssd_d, ssd_norm_w, w_out, post_mix_norm, pre_ffn_norm, w_up, ffn_conv_w, ffn_conv_b,
              w_down, post_ffn_norm)
    for layer in range(pre_mix_norm.shape[0]):
        x = _layer(x, *(p[layer] for p in params))
    return x
```

```python
import functools

import jax
import jax.numpy as jnp
import numpy as np
from jax import lax
from jax.experimental import pallas as pl
from jax.experimental.pallas import tpu as pltpu

F32 = jnp.float32
BF16 = jnp.bfloat16

D_MODEL = 1024
GDN_HEADS = 8
GDN_DK = 128
GDN_DV = 128
SSD_HEADS = 16
SSD_HEADDIM = 64
SSD_GROUPS = 2
SSD_STATE = 128
CONV_K = 4
CHUNK = 64
D_FF = 2816
FFN_CONV_K = 3
EPS = 1e-6

GDN_QK = GDN_HEADS * GDN_DK
GDN_V = GDN_HEADS * GDN_DV
SSD_D = SSD_HEADS * SSD_HEADDIM
SSD_BC = SSD_GROUPS * SSD_STATE
MIX_WIDTH = GDN_V + SSD_D
GDN_CONV_CH = 2 * GDN_QK + GDN_V
SSD_CONV_CH = SSD_D + 2 * SSD_BC
CONV_CH = GDN_CONV_CH + SSD_CONV_CH
IN_SPLITS = (GDN_QK, GDN_QK, GDN_V, GDN_V, GDN_HEADS, GDN_HEADS, SSD_D, SSD_D, SSD_BC, SSD_BC, SSD_HEADS)

LANES = 128
SUBLANES = 8
VMEM_LIMIT_BYTES = 56 * 1024 * 1024

COL_XS = GDN_CONV_CH
COL_B = COL_XS + SSD_D
COL_C = COL_B + SSD_BC
COL_ZA = CONV_CH
COL_ZS = COL_ZA + GDN_V
COL_SMALL = COL_ZS + SSD_D
W_IN_COLS = COL_SMALL + LANES
LANE_G = 0
LANE_DT = GDN_HEADS
LANE_BETA = GDN_HEADS + SSD_HEADS
N_GATE_LANES = GDN_HEADS + SSD_HEADS

MIX_TILE = 256
FFN_TILE = 1024
PERM_BLOCK = 256
MIX_CW = 256
FFN_CW = 256
FFN_SUB = 4
GROUP = 4
RB = CHUNK // SUBLANES
assert RB == SUBLANES


def _bf(x):
    return x.astype(BF16)


def _dot(a, b):
    return jnp.dot(_bf(a), _bf(b), preferred_element_type=F32)


def _dot_nt(a, b):
    return lax.dot_general(_bf(a), _bf(b), (((1,), (1,)), ((), ())), preferred_element_type=F32)


def _rmsnorm(x, w):
    return x * lax.rsqrt(jnp.mean(x * x, axis=-1, keepdims=True) + EPS) * w


def _sigmoid(x):
    return 1.0 / (1.0 + jnp.exp(-x))


def _silu(x):
    return x * _sigmoid(x)


def _softplus(x):
    return jnp.maximum(x, 0.0) + jnp.log1p(jnp.exp(-jnp.abs(x)))


def _row_time(idx):
    return ((idx & (SUBLANES - 1)) << 3) | (idx >> 3)


def _perm_matrix(rows):
    r = np.arange(rows)
    t = (r // CHUNK) * CHUNK + ((r % SUBLANES) * RB + (r % CHUNK) // SUBLANES)
    p = np.zeros((rows, rows), np.float32)
    p[r, t] = 1.0
    return jnp.asarray(p, BF16)


def _to_chunk_order(hb, perm_ref):
    tile = hb.shape[0]
    parts = [jnp.dot(perm_ref[...], hb[r:r + PERM_BLOCK], preferred_element_type=F32).astype(BF16)
             for r in range(0, tile, PERM_BLOCK)]
    return parts[0] if len(parts) == 1 else jnp.concatenate(parts, axis=0)


def _store_natural(o_ref, x_ref, rn, slab_s, row0):
    n = rn.shape[0]
    for lt in range(D_MODEL // LANES):
        slab_s[lt, row0:row0 + n, :] = rn[:, lt * LANES:(lt + 1) * LANES]
    for lt in range(D_MODEL // LANES):
        cols = slice(lt * LANES, (lt + 1) * LANES)
        for c in range(n // CHUNK):
            for s in range(SUBLANES):
                r0 = row0 + c * CHUNK + s * RB
                o_ref[0, r0:r0 + RB, cols] = (x_ref[0, r0:r0 + RB, cols]
                                              + slab_s[lt, pl.ds(row0 + c * CHUNK + s, RB, stride=SUBLANES), :])


def _conv_chunks(streams, carry_ref):
    tile = streams[0][0].shape[0]
    nk = len(streams[0][1])
    sub = lax.broadcasted_iota(jnp.int32, (SUBLANES, streams[0][0].shape[1]), 0)
    prev = [[carry_ref[j * SUBLANES:(j + 1) * SUBLANES, cols] for j in range(nk - 1)] for _, _, _, cols in streams]
    for c in range(tile // CHUNK):
        exts = []
        for si, (u, _, _, _) in enumerate(streams):
            x = [u[c * CHUNK + i * SUBLANES:c * CHUNK + (i + 1) * SUBLANES] for i in range(RB)]
            z = [pltpu.roll(jnp.where(sub == SUBLANES - 1, prev[si][j], x[RB - (nk - 1) + j]), 1, axis=0)
                 for j in range(nk - 1)]
            exts.append(z + x)
            prev[si] = x[RB - (nk - 1):]
        for i in range(RB):
            accs = []
            for (_, taps, bias, _), ext in zip(streams, exts):
                acc = bias + taps[0] * ext[i]
                for k in range(1, nk):
                    acc = acc + taps[k] * ext[i + k]
                accs.append(acc)
            yield c * RB + i, accs
    for (_, _, _, cols), pv in zip(streams, prev):
        for j in range(nk - 1):
            carry_ref[j * SUBLANES:(j + 1) * SUBLANES, cols] = pv[j]


def _chunk_cumsum(x):
    p = [x[0:SUBLANES]]
    for i in range(1, RB):
        p.append(p[-1] + x[i * SUBLANES:(i + 1) * SUBLANES])
    tot = p[-1]
    sub = lax.broadcasted_iota(jnp.int32, tot.shape, 0)
    inc = tot
    s = 1
    while s < SUBLANES:
        inc = inc + jnp.where(sub >= s, pltpu.roll(inc, s, axis=0), 0.0)
        s *= 2
    exc = inc - tot
    return jnp.concatenate([pi + exc for pi in p], axis=0)


def _unit_lower_inverse(a, eye, time_r, time_c, mm, fill):
    hs = range(len(a))

    def coupling(level):
        return ((time_r >> (level + 1)) == (time_c >> (level + 1))) & ((time_r >> level) != (time_c >> level))

    t = [eye - jnp.where(coupling(0), a[h], 0.0) for h in hs]
    for level in range(1, CHUNK.bit_length() - 1):
        mask = coupling(level)
        x = [mm(jnp.where(mask, a[h], 0.0), t[h]) for h in hs]
        fill()
        t = [t[h] - mm(t[h], x[h]) for h in hs]
        fill()
    return t


def _mixer_kernel(x_ref, perm_ref, nw_ref, win_ref, cw_ref, cb_ref, sp_ref, gnw_ref, dsk_ref, snw_ref, wout_ref,
                  pnw_ref, o_ref, h_s, carry_s, act_s, col_s, oy_s, z_s, cat_s, slab_s, gst_s, sst_s):
    tile = x_ref.shape[1]
    n_groups = tile // (CHUNK * GROUP)

    @pl.when(pl.program_id(1) == 0)
    def _():
        carry_s[...] = jnp.zeros(carry_s.shape, F32)
        gst_s[...] = jnp.zeros(gst_s.shape, F32)
        sst_s[...] = jnp.zeros(sst_s.shape, F32)

    h_s[...] = _to_chunk_order(_bf(_rmsnorm(x_ref[0], nw_ref[...])), perm_ref)

    for cbk in range(CONV_CH // MIX_CW):
        cols = slice(cbk * MIX_CW, (cbk + 1) * MIX_CW)
        pre = jnp.dot(h_s[...], win_ref[:, cols], preferred_element_type=F32)
        taps = [jnp.broadcast_to(cw_ref[k:k + 1, cols], (SUBLANES, MIX_CW)) for k in range(CONV_K)]
        bias = jnp.broadcast_to(cb_ref[:, cols], (SUBLANES, MIX_CW))
        if cbk * MIX_CW < 2 * GDN_QK:
            scale = GDN_DK ** -0.5 if cbk * MIX_CW < GDN_QK else 1.0
        else:
            scale = None
        for r, (acc,) in _conv_chunks([(pre, taps, bias, cols)], carry_s):
            a = _silu(acc)
            if scale is not None:
                parts = []
                for hh in range(MIX_CW // GDN_DK):
                    v = a[:, hh * GDN_DK:(hh + 1) * GDN_DK]
                    parts.append(v * (lax.rsqrt(jnp.sum(v * v, axis=-1, keepdims=True) + EPS) * scale))
                a = jnp.concatenate(parts, axis=1)
            act_s[r * SUBLANES:(r + 1) * SUBLANES, cols] = a

    small = jnp.dot(h_s[...], win_ref[:, COL_SMALL:COL_SMALL + LANES], preferred_element_type=F32)
    lane1 = lax.broadcasted_iota(jnp.int32, (1, LANES), 1)
    a_row = jnp.where(lane1 < N_GATE_LANES, -jnp.exp(sp_ref[1:2, :]), 0.0)
    sp = _softplus(small + sp_ref[0:1, :])
    col_s[0] = sp * a_row
    col_s[1] = sp
    col_s[2] = pltpu.roll(_sigmoid(small), LANES - LANE_BETA, axis=1)

    z_s[:, 0:GDN_V] = jnp.dot(h_s[...], win_ref[:, COL_ZA:COL_ZA + GDN_V], preferred_element_type=F32)
    z_s[:, GDN_V:MIX_WIDTH] = jnp.dot(h_s[...], win_ref[:, COL_ZS:COL_ZS + SSD_D], preferred_element_type=F32)

    r_t = _row_time(lax.broadcasted_iota(jnp.int32, (CHUNK, CHUNK), 0))
    c_t = _row_time(lax.broadcasted_iota(jnp.int32, (CHUNK, CHUNK), 1))
    tril = r_t >= c_t
    lane_cl = lax.broadcasted_iota(jnp.int32, (CHUNK, LANES), 1)
    r_t2 = _row_time(lax.broadcasted_iota(jnp.int32, (CHUNK, LANES), 0))
    c_t2 = _row_time(lane_cl & (CHUNK - 1))
    tril2 = r_t2 >= c_t2
    strict2 = r_t2 > c_t2
    eye2 = (r_t2 == c_t2).astype(F32)
    first2 = lane_cl < CHUNK
    row_ll = lax.broadcasted_iota(jnp.int32, (LANES, LANES), 0)
    lane_ll = lax.broadcasted_iota(jnp.int32, (LANES, LANES), 1)
    chunk_shift = CHUNK.bit_length() - 1
    diag_blocks = (row_ll >> chunk_shift) == (lane_ll >> chunk_shift)
    zeros_k = jnp.zeros((CHUNK, GDN_DK), BF16)
    zeros_2k = jnp.zeros((CHUNK, 2 * GDN_DK), BF16)

    def pair_mm(x, y):
        yb = _bf(y)
        rhs_bd = jnp.where(diag_blocks, jnp.concatenate([yb, yb], axis=0), 0.0)
        return jnp.dot(_bf(x), rhs_bd, preferred_element_type=F32)

    def pair_rows(t2, l0, l1):
        return jnp.where(lane1 < CHUNK, t2[l0:l0 + 1, :], t2[l1:l1 + 1, :])

    def pair_cols(c, l0, l1):
        return jnp.where(first2, c[:, l0:l0 + 1], c[:, l1:l1 + 1])

    is_g = lane_cl < GDN_HEADS
    neg_inf = -jnp.inf
    n_pairs = SSD_HEADS // 2
    pairs_per_group = n_pairs // SSD_GROUPS
    hs = range(GDN_HEADS)

    def group_body(gi, carry):
        base = gi * (CHUNK * GROUP)
        rows = [pl.ds(pl.multiple_of(base + cc * CHUNK, CHUNK), CHUNK) for cc in range(GROUP)]

        cs, ecs, gl, beta0, cs_t, bd_t, edl_t, edc = [], [], [], [], [], [], [], []
        for cc in range(GROUP):
            gad = col_s[0, rows[cc], :]
            dtv = col_s[1, rows[cc], :]
            b0 = col_s[2, rows[cc], :]
            c_ = _chunk_cumsum(gad)
            last = c_[CHUNK - 1:CHUNK, :]
            e_ = jnp.exp(c_)
            edl = jnp.exp(last - c_)
            cs.append(c_)
            ecs.append(e_)
            gl.append(jnp.exp(last))
            beta0.append(b0)
            cs_t.append(jnp.concatenate([c_, c_], axis=0).T)
            bd_ = jnp.where(is_g, b0, dtv)
            bd_t.append(jnp.concatenate([bd_, bd_], axis=0).T)
            edl_t.append(edl.T)
            edc.append(edl * dtv)

        sst = [sst_s[p] for p in range(n_pairs)]
        prep_jobs, y_jobs, s_jobs = [], [], []
        for cc in range(GROUP):
            cb, b_t, c_g = [], [], []
            ys = [None] * SSD_HEADS
            xp = [None] * n_pairs
            rhs_y = [None] * n_pairs

            def prep_job(cc=cc, cb=cb, b_t=b_t, c_g=c_g, xp=xp):
                for g in range(SSD_GROUPS):
                    bm = act_s[rows[cc], COL_B + g * SSD_STATE:COL_B + (g + 1) * SSD_STATE]
                    cm = act_s[rows[cc], COL_C + g * SSD_STATE:COL_C + (g + 1) * SSD_STATE]
                    cb.append(_dot_nt(cm, bm))
                    b_t.append(_bf(bm.T))
                    c_g.append(cm)
                for p in range(n_pairs):
                    xp[p] = act_s[rows[cc], COL_XS + p * LANES:COL_XS + (p + 1) * LANES]

            def y_job(i, cc=cc, cb=cb, c_g=c_g, xp=xp, ys=ys, rhs_y=rhs_y):
                p, g, ln = i // 2, i // 2 // pairs_per_group, LANE_DT + i
                if i % 2 == 0:
                    rhs_y[p] = jnp.concatenate([_bf(sst[p]), _bf(xp[p])], axis=0)
                lmat = jnp.exp(jnp.where(tril, cs[cc][:, ln:ln + 1] - cs_t[cc][ln:ln + 1, 0:CHUNK], neg_inf))
                mh = cb[g] * lmat * bd_t[cc][ln:ln + 1, 0:CHUNK]
                ce = c_g[g] * ecs[cc][:, ln:ln + 1]
                ys[i] = jnp.dot(jnp.concatenate([_bf(ce), _bf(mh)], axis=1), rhs_y[p], preferred_element_type=F32)

            def s_job(p, cc=cc, b_t=b_t, xp=xp, ys=ys):
                g, l0, l1 = p // pairs_per_group, LANE_DT + 2 * p, LANE_DT + 2 * p + 1
                oy_s[rows[cc], GDN_V + p * LANES:GDN_V + (p + 1) * LANES] = jnp.where(
                    lane_cl < SSD_HEADDIM, ys[2 * p], ys[2 * p + 1])
                coef = jnp.where(lane_cl < SSD_HEADDIM, edc[cc][:, l0:l0 + 1], edc[cc][:, l1:l1 + 1])
                glp = jnp.where(lane1 < SSD_HEADDIM, gl[cc][:, l0:l0 + 1], gl[cc][:, l1:l1 + 1])
                sst[p] = sst[p] * glp + _dot(b_t[g], xp[p] * coef)

            prep_jobs.append(prep_job)
            y_jobs.append([functools.partial(y_job, i) for i in range(SSD_HEADS)])
            s_jobs.append([functools.partial(s_job, p) for p in range(n_pairs)])

        def fill():
            if prep_jobs:
                prep_jobs.pop(0)()

        hp_n = GDN_HEADS // 2
        ch = [(cc, hp) for cc in range(GROUP) for hp in range(hp_n)]
        nch = range(len(ch))

        def head_cols(ref_base, cc, h):
            return act_s[rows[cc], ref_base + h * GDN_DK:ref_base + (h + 1) * GDN_DK]

        q = [[head_cols(0, cc, 2 * hp + e) for e in range(2)] for cc, hp in ch]
        k = [[head_cols(GDN_QK, cc, 2 * hp + e) for e in range(2)] for cc, hp in ch]
        v = [[head_cols(2 * GDN_QK, cc, 2 * hp + e) for e in range(2)] for cc, hp in ch]
        kb = [[_bf(k[i][e]) for e in range(2)] for i in nch]
        kq = []
        for i in nch:
            lhs_kq = jnp.concatenate([jnp.concatenate(kb[i], axis=1),
                                      jnp.concatenate([_bf(q[i][0]), _bf(q[i][1])], axis=1)], axis=0)
            rhs_kq = jnp.concatenate([jnp.concatenate([kb[i][0], zeros_k], axis=1),
                                      jnp.concatenate([zeros_k, kb[i][1]], axis=1)], axis=0)
            kq.append(_dot_nt(lhs_kq, rhs_kq))
        fill()
        dec = [jnp.exp(jnp.where(tril2, pair_cols(cs[cc], 2 * hp, 2 * hp + 1)
                                 - pair_rows(cs_t[cc], 2 * hp, 2 * hp + 1), neg_inf)) for cc, hp in ch]
        a = [jnp.where(strict2, kq[i][0:CHUNK] * pair_cols(beta0[cc], 2 * hp, 2 * hp + 1) * dec[i], 0.0)
             for i, (cc, hp) in enumerate(ch)]
        tinv = _unit_lower_inverse(a, eye2, r_t2, c_t2, pair_mm, fill)
        sol = []
        for i, (cc, hp) in enumerate(ch):
            rhs_h = [jnp.concatenate([_bf(v[i][e]), _bf(k[i][e] * ecs[cc][:, 2 * hp + e:2 * hp + e + 1])], axis=1)
                     for e in range(2)]
            rhs_bd = jnp.concatenate([jnp.concatenate([rhs_h[0], zeros_2k], axis=1),
                                      jnp.concatenate([zeros_2k, rhs_h[1]], axis=1)], axis=0)
            sol.append(_dot(tinv[i] * pair_rows(bd_t[cc], 2 * hp, 2 * hp + 1), rhs_bd))
        fill()
        qkd = [_bf(kq[i][CHUNK:] * dec[i]) for i in nch]

        gst = [gst_s[h] for h in hs]
        sol_w = 2 * GDN_DV
        for cc in range(GROUP):
            wq, u = [], []
            for h in hs:
                i, e = cc * hp_n + h // 2, h % 2
                u.append(sol[i][:, e * sol_w:e * sol_w + GDN_DV])
                wq.append(jnp.concatenate([_bf(sol[i][:, e * sol_w + GDN_DV:(e + 1) * sol_w]),
                                           _bf(q[i][e] * ecs[cc][:, h:h + 1])], axis=0))
            ws = [_dot(wq[h], gst[h]) for h in hs]
            for job in y_jobs[cc]:
                job()
            v_new = [_bf(u[h] - ws[h][0:CHUNK]) for h in hs]
            oqk = []
            for hp in range(hp_n):
                vn_bd = jnp.concatenate([jnp.concatenate([v_new[2 * hp], zeros_k], axis=1),
                                         jnp.concatenate([zeros_k, v_new[2 * hp + 1]], axis=1)], axis=0)
                oqk.append(jnp.dot(qkd[cc * hp_n + hp], vn_bd, preferred_element_type=F32))
            upd = [_dot(k[cc * hp_n + h // 2][h % 2].T * edl_t[cc][h:h + 1, :], v_new[h]) for h in hs]
            for job in s_jobs[cc]:
                job()
            for h in hs:
                oy_s[rows[cc], h * GDN_DV:(h + 1) * GDN_DV] = (
                    ws[h][CHUNK:] + oqk[h // 2][:, (h % 2) * GDN_DV:(h % 2 + 1) * GDN_DV])
                gst[h] = gst[h] * gl[cc][:, h:h + 1] + upd[h]
        for h in hs:
            gst_s[h] = gst[h]
        for p in range(n_pairs):
            sst_s[p] = sst[p]
        return carry

    lax.fori_loop(0, n_groups, group_body, 0)

    for hd in range(GDN_HEADS):
        cols = slice(hd * GDN_DV, (hd + 1) * GDN_DV)
        o = oy_s[:, cols]
        o = o * lax.rsqrt(jnp.mean(o * o, axis=-1, keepdims=True) + EPS) * gnw_ref[:, cols]
        cat_s[:, cols] = _bf(o * _silu(z_s[:, cols]))
    gw = SSD_D // SSD_GROUPS
    for g in range(SSD_GROUPS):
        cols = slice(GDN_V + g * gw, GDN_V + (g + 1) * gw)
        xs = act_s[:, COL_XS + g * gw:COL_XS + (g + 1) * gw]
        y = oy_s[:, cols] + dsk_ref[:, g * gw:(g + 1) * gw] * xs
        y = y * _silu(z_s[:, cols])
        y = y * lax.rsqrt(jnp.mean(y * y, axis=-1, keepdims=True) + EPS) * snw_ref[:, g * gw:(g + 1) * gw]
        cat_s[:, cols] = _bf(y)

    mix = jnp.dot(cat_s[...], wout_ref[...], preferred_element_type=F32)
    _store_natural(o_ref, x_ref, _rmsnorm(mix, pnw_ref[...]), slab_s, 0)


def _ffn_kernel(x_ref, perm_ref, nw_ref, wup_ref, cw_ref, cb_ref, wdn_ref, pnw_ref, o_ref, h_s, carry_s, act_s,
                slab_s):
    tile = x_ref.shape[1]
    sub_rows = tile // FFN_SUB

    @pl.when(pl.program_id(1) == 0)
    def _():
        carry_s[...] = jnp.zeros(carry_s.shape, F32)

    def prenorm(sub):
        rows = slice(sub * sub_rows, (sub + 1) * sub_rows)
        h_s[rows, :] = _to_chunk_order(_bf(_rmsnorm(x_ref[0, rows, :], nw_ref[...])), perm_ref)

    prenorm(0)
    for sub in range(FFN_SUB):
        rows = slice(sub * sub_rows, (sub + 1) * sub_rows)
        for c in range(D_FF // FFN_CW):
            streams = []
            for base in (0, D_FF):
                cols = slice(base + c * FFN_CW, base + (c + 1) * FFN_CW)
                u = jnp.dot(h_s[rows, :], wup_ref[:, cols], preferred_element_type=F32)
                taps = [jnp.broadcast_to(cw_ref[k:k + 1, cols], (SUBLANES, FFN_CW)) for k in range(FFN_CONV_K)]
                streams.append((u, taps, jnp.broadcast_to(cb_ref[:, cols], (SUBLANES, FFN_CW)), cols))
            for r, (g, up) in _conv_chunks(streams, carry_s):
                r0 = sub * sub_rows + r * SUBLANES
                act_s[r0:r0 + SUBLANES, c * FFN_CW:(c + 1) * FFN_CW] = _bf(_silu(g) * up)
        if sub + 1 < FFN_SUB:
            prenorm(sub + 1)
        f = jnp.dot(act_s[rows, :], wdn_ref[...], preferred_element_type=F32)
        _store_natural(o_ref, x_ref, _rmsnorm(f, pnw_ref[...]), slab_s, sub * sub_rows)


def _const_spec(shape):
    return pl.BlockSpec(shape, lambda b, j: (0,) * len(shape), pipeline_mode=pl.Buffered(1))


def _mixer_call(x, nw, win, cw, cb, sp, gnw, dsk, snw, wout, pnw):
    bsz, seq, _ = x.shape
    tile = MIX_TILE
    assert seq % tile == 0 and tile % (CHUNK * GROUP) == 0 and tile % PERM_BLOCK == 0
    tok_spec = pl.BlockSpec((1, tile, D_MODEL), lambda b, j: (b, j, 0))
    consts = (_perm_matrix(PERM_BLOCK), nw, win, cw, cb, sp, gnw, dsk, snw, wout, pnw)
    return pl.pallas_call(
        _mixer_kernel,
        grid=(bsz, seq // tile),
        in_specs=[tok_spec] + [_const_spec(c.shape) for c in consts],
        out_specs=tok_spec,
        out_shape=jax.ShapeDtypeStruct(x.shape, x.dtype),
        scratch_shapes=[
            pltpu.VMEM((tile, D_MODEL), BF16),
            pltpu.VMEM(((CONV_K - 1) * SUBLANES, CONV_CH), F32),
            pltpu.VMEM((tile, CONV_CH), F32),
            pltpu.VMEM((3, tile, LANES), F32),
            pltpu.VMEM((tile, MIX_WIDTH), F32),
            pltpu.VMEM((tile, MIX_WIDTH), F32),
            pltpu.VMEM((tile, MIX_WIDTH), BF16),
            pltpu.VMEM((D_MODEL // LANES, tile, LANES), F32),
            pltpu.VMEM((GDN_HEADS, GDN_DK, GDN_DV), F32),
            pltpu.VMEM((SSD_HEADS // 2, SSD_STATE, LANES), F32),
        ],
        compiler_params=pltpu.CompilerParams(
            dimension_semantics=("arbitrary", "arbitrary"), vmem_limit_bytes=VMEM_LIMIT_BYTES),
        name="mixer",
    )(x, *consts)


def _ffn_call(x, nw, wup, cw, cb, wdn, pnw):
    bsz, seq, _ = x.shape
    tile = FFN_TILE
    assert seq % tile == 0 and tile % (FFN_SUB * PERM_BLOCK) == 0
    tok_spec = pl.BlockSpec((1, tile, D_MODEL), lambda b, j: (b, j, 0))
    consts = (_perm_matrix(PERM_BLOCK), nw, wup, cw, cb, wdn, pnw)
    return pl.pallas_call(
        _ffn_kernel,
        grid=(bsz, seq // tile),
        in_specs=[tok_spec] + [_const_spec(c.shape) for c in consts],
        out_specs=tok_spec,
        out_shape=jax.ShapeDtypeStruct(x.shape, x.dtype),
        scratch_shapes=[
            pltpu.VMEM((tile, D_MODEL), BF16),
            pltpu.VMEM(((FFN_CONV_K - 1) * SUBLANES, 2 * D_FF), F32),
            pltpu.VMEM((tile, D_FF), BF16),
            pltpu.VMEM((D_MODEL // LANES, tile, LANES), F32),
        ],
        compiler_params=pltpu.CompilerParams(
            dimension_semantics=("arbitrary", "arbitrary"), vmem_limit_bytes=VMEM_LIMIT_BYTES),
        name="ffn",
    )(x, *consts)


def _row(v):
    return v.reshape(1, -1).astype(F32)


def _layer(x, pre_mix_norm, w_in, gdn_conv_w, gdn_a_log, gdn_dt_bias, gdn_norm_w, ssd_conv_w, ssd_conv_b,
           ssd_a_log, ssd_dt_bias, ssd_d, ssd_norm_w, w_out, post_mix_norm, pre_ffn_norm, w_up, ffn_conv_w,
           ffn_conv_b, w_down, post_ffn_norm):
    offs = [0]
    for n in IN_SPLITS:
        offs.append(offs[-1] + n)
    seg = [w_in[:, offs[i]:offs[i + 1]] for i in range(len(IN_SPLITS))]
    wq, wk, wv, wza, wb, wa, wzs, wxs, wbs, wcs, wdt = seg
    pad = jnp.zeros((D_MODEL, LANES - N_GATE_LANES - GDN_HEADS), w_in.dtype)
    win = _bf(jnp.concatenate([wq, wk, wv, wxs, wbs, wcs, wza, wzs, wa, wdt, wb, pad], axis=1))
    cw = jnp.concatenate([gdn_conv_w, ssd_conv_w], axis=1).astype(F32)
    cb = jnp.concatenate([jnp.zeros((GDN_CONV_CH,), F32), ssd_conv_b.astype(F32)]).reshape(1, CONV_CH)
    zpad = jnp.zeros((LANES - N_GATE_LANES,), F32)
    sp = jnp.zeros((SUBLANES, LANES), F32)
    sp = sp.at[0].set(jnp.concatenate([gdn_dt_bias.astype(F32), ssd_dt_bias.astype(F32), zpad]))
    sp = sp.at[1].set(jnp.concatenate([gdn_a_log.astype(F32), ssd_a_log.astype(F32), zpad]))
    gnw = _row(jnp.tile(gdn_norm_w, GDN_HEADS))
    dsk = _row(jnp.repeat(ssd_d, SSD_HEADDIM))
    x1 = _mixer_call(x, _row(pre_mix_norm), win, cw, cb, sp, gnw, dsk, _row(ssd_norm_w), _bf(w_out),
                     _row(post_mix_norm))
    return _ffn_call(x1, _row(pre_ffn_norm), _bf(w_up), ffn_conv_w.astype(F32), _row(ffn_conv_b), _bf(w_down),
                     _row(post_ffn_norm))


def kernel(x, pre_mix_norm, w_in, gdn_conv_w, gdn_a_log, gdn_dt_bias, gdn_norm_w, ssd_conv_w, ssd_conv_b, ssd_a_log, ssd_dt_bias, ssd_d, ssd_norm_w, w_out, post_mix_norm, pre_ffn_norm, w_up, ffn_conv_w, ffn_conv_b, w_down, post_ffn_norm):
    params = (pre_mix_norm, w_in, gdn_conv_w, gdn_a_log, gdn_dt_bias, gdn_norm_w, ssd_conv_w, ssd_conv_b, ssd_a_log,
              ssd_dt_bias, ssd_d, ssd_norm_w, w_out, post_mix_norm, pre_ffn_norm, w_up, ffn_conv_w, ffn_conv_b,
              w_down, post_ffn_norm)
    for layer in range(pre_mix_norm.shape[0]):
        x = _layer(x, *(p[layer] for p in params))
    return x
```

```python
import functools

import jax
import jax.numpy as jnp
import numpy as np
from jax import lax
from jax.experimental import pallas as pl
from jax.experimental.pallas import tpu as pltpu

F32 = jnp.float32
BF16 = jnp.bfloat16

D_MODEL = 1024
GDN_HEADS = 8
GDN_DK = 128
GDN_DV = 128
SSD_HEADS = 16
SSD_HEADDIM = 64
SSD_GROUPS = 2
SSD_STATE = 128
CONV_K = 4
CHUNK = 64
D_FF = 2816
FFN_CONV_K = 3
EPS = 1e-6

GDN_QK = GDN_HEADS * GDN_DK
GDN_V = GDN_HEADS * GDN_DV
SSD_D = SSD_HEADS * SSD_HEADDIM
SSD_BC = SSD_GROUPS * SSD_STATE
MIX_WIDTH = GDN_V + SSD_D
GDN_CONV_CH = 2 * GDN_QK + GDN_V
SSD_CONV_CH = SSD_D + 2 * SSD_BC
CONV_CH = GDN_CONV_CH + SSD_CONV_CH
IN_SPLITS = (GDN_QK, GDN_QK, GDN_V, GDN_V, GDN_HEADS, GDN_HEADS, SSD_D, SSD_D, SSD_BC, SSD_BC, SSD_HEADS)

LANES = 128
SUBLANES = 8
VMEM_LIMIT_BYTES = 56 * 1024 * 1024

COL_XS = GDN_CONV_CH
COL_B = COL_XS + SSD_D
COL_C = COL_B + SSD_BC
LANE_G = 0
LANE_DT = GDN_HEADS
LANE_BETA = GDN_HEADS + SSD_HEADS
N_GATE_LANES = GDN_HEADS + SSD_HEADS

MIX_TILE = 256
FFN_TILE = 1024
PERM_BLOCK = 256
MIX_CW = 256
FFN_CW = 256
FFN_SUB = 4
GROUP = 4
RB = CHUNK // SUBLANES
assert RB == SUBLANES


def _bf(x):
    return x.astype(BF16)


def _dot(a, b):
    return jnp.dot(_bf(a), _bf(b), preferred_element_type=F32)


def _dot_nt(a, b):
    return lax.dot_general(_bf(a), _bf(b), (((1,), (1,)), ((), ())), preferred_element_type=F32)


def _rmsnorm(x, w):
    return x * lax.rsqrt(jnp.mean(x * x, axis=-1, keepdims=True) + EPS) * w


def _sigmoid(x):
    return 1.0 / (1.0 + jnp.exp(-x))


def _silu(x):
    return x * _sigmoid(x)


def _softplus(x):
    return jnp.maximum(x, 0.0) + jnp.log1p(jnp.exp(-jnp.abs(x)))


def _row_time(idx):
    return ((idx & (SUBLANES - 1)) << 3) | (idx >> 3)


def _perm_matrix(rows):
    r = np.arange(rows)
    t = (r // CHUNK) * CHUNK + ((r % SUBLANES) * RB + (r % CHUNK) // SUBLANES)
    p = np.zeros((rows, rows), np.float32)
    p[r, t] = 1.0
    return jnp.asarray(p, BF16)


def _to_chunk_order(hb, perm_ref):
    tile = hb.shape[0]
    parts = [jnp.dot(perm_ref[...], hb[r:r + PERM_BLOCK], preferred_element_type=F32).astype(BF16)
             for r in range(0, tile, PERM_BLOCK)]
    return parts[0] if len(parts) == 1 else jnp.concatenate(parts, axis=0)


def _store_natural(o_ref, x_ref, rn, slab_s, row0):
    n = rn.shape[0]
    for lt in range(D_MODEL // LANES):
        slab_s[lt, row0:row0 + n, :] = rn[:, lt * LANES:(lt + 1) * LANES]
    for lt in range(D_MODEL // LANES):
        cols = slice(lt * LANES, (lt + 1) * LANES)
        for c in range(n // CHUNK):
            for s in range(SUBLANES):
                r0 = row0 + c * CHUNK + s * RB
                o_ref[0, r0:r0 + RB, cols] = (x_ref[0, r0:r0 + RB, cols]
                                              + slab_s[lt, pl.ds(row0 + c * CHUNK + s, RB, stride=SUBLANES), :])


def _conv_chunks(streams, carry_ref):
    tile = streams[0][0].shape[0]
    nk = len(streams[0][1])
    sub = lax.broadcasted_iota(jnp.int32, (SUBLANES, streams[0][0].shape[1]), 0)
    prev = [[carry_ref[j * SUBLANES:(j + 1) * SUBLANES, cols] for j in range(nk - 1)] for _, _, _, cols in streams]
    for c in range(tile // CHUNK):
        exts = []
        for si, (u, _, _, _) in enumerate(streams):
            x = [u[c * CHUNK + i * SUBLANES:c * CHUNK + (i + 1) * SUBLANES] for i in range(RB)]
            z = [pltpu.roll(jnp.where(sub == SUBLANES - 1, prev[si][j], x[RB - (nk - 1) + j]), 1, axis=0)
                 for j in range(nk - 1)]
            exts.append(z + x)
            prev[si] = x[RB - (nk - 1):]
        for i in range(RB):
            accs = []
            for (_, taps, bias, _), ext in zip(streams, exts):
                acc = bias + taps[0] * ext[i]
                for k in range(1, nk):
                    acc = acc + taps[k] * ext[i + k]
                accs.append(acc)
            yield c * RB + i, accs
    for (_, _, _, cols), pv in zip(streams, prev):
        for j in range(nk - 1):
            carry_ref[j * SUBLANES:(j + 1) * SUBLANES, cols] = pv[j]


def _chunk_cumsum(x):
    p = [x[0:SUBLANES]]
    for i in range(1, RB):
        p.append(p[-1] + x[i * SUBLANES:(i + 1) * SUBLANES])
    tot = p[-1]
    sub = lax.broadcasted_iota(jnp.int32, tot.shape, 0)
    inc = tot
    s = 1
    while s < SUBLANES:
        inc = inc + jnp.where(sub >= s, pltpu.roll(inc, s, axis=0), 0.0)
        s *= 2
    exc = inc - tot
    return jnp.concatenate([pi + exc for pi in p], axis=0)


def _unit_lower_inverse(a, eye, time_r, time_c, mm, fill):
    hs = range(len(a))

    def coupling(level):
        return ((time_r >> (level + 1)) == (time_c >> (level + 1))) & ((time_r >> level) != (time_c >> level))

    t = [eye - jnp.where(coupling(0), a[h], 0.0) for h in hs]
    for level in range(1, CHUNK.bit_length() - 1):
        mask = coupling(level)
        x = [mm(jnp.where(mask, a[h], 0.0), t[h]) for h in hs]
        fill()
        t = [t[h] - mm(t[h], x[h]) for h in hs]
        fill()
    return t


def _mixer_kernel(x_ref, perm_ref, nw_ref, wqkv_ref, wxbc_ref, wza_ref, wzs_ref, wsm_ref, cw_ref, cb_ref, sp_ref,
                  gnw_ref, dsk_ref, snw_ref, wout_ref,
                  pnw_ref, o_ref, h_s, carry_s, act_s, col_s, oy_s, z_s, cat_s, slab_s, gst_s, sst_s):
    tile = x_ref.shape[1]
    n_groups = tile // (CHUNK * GROUP)

    @pl.when(pl.program_id(1) == 0)
    def _():
        carry_s[...] = jnp.zeros(carry_s.shape, F32)
        gst_s[...] = jnp.zeros(gst_s.shape, F32)
        sst_s[...] = jnp.zeros(sst_s.shape, F32)

    h_s[...] = _to_chunk_order(_bf(_rmsnorm(x_ref[0], nw_ref[...])), perm_ref)

    for cbk in range(CONV_CH // MIX_CW):
        cols = slice(cbk * MIX_CW, (cbk + 1) * MIX_CW)
        if cbk * MIX_CW < GDN_CONV_CH:
            w_blk = wqkv_ref[:, cols]
        else:
            w_blk = wxbc_ref[:, cbk * MIX_CW - GDN_CONV_CH:(cbk + 1) * MIX_CW - GDN_CONV_CH]
        pre = jnp.dot(h_s[...], w_blk, preferred_element_type=F32)
        taps = [jnp.broadcast_to(cw_ref[k:k + 1, cols], (SUBLANES, MIX_CW)) for k in range(CONV_K)]
        bias = jnp.broadcast_to(cb_ref[:, cols], (SUBLANES, MIX_CW))
        if cbk * MIX_CW < 2 * GDN_QK:
            scale = GDN_DK ** -0.5 if cbk * MIX_CW < GDN_QK else 1.0
        else:
            scale = None
        for r, (acc,) in _conv_chunks([(pre, taps, bias, cols)], carry_s):
            a = _silu(acc)
            if scale is not None:
                parts = []
                for hh in range(MIX_CW // GDN_DK):
                    v = a[:, hh * GDN_DK:(hh + 1) * GDN_DK]
                    parts.append(v * (lax.rsqrt(jnp.sum(v * v, axis=-1, keepdims=True) + EPS) * scale))
                a = jnp.concatenate(parts, axis=1)
            act_s[r * SUBLANES:(r + 1) * SUBLANES, cols] = a

    small = jnp.dot(h_s[...], wsm_ref[...], preferred_element_type=F32)
    lane1 = lax.broadcasted_iota(jnp.int32, (1, LANES), 1)
    a_row = jnp.where(lane1 < N_GATE_LANES, -jnp.exp(sp_ref[1:2, :]), 0.0)
    sp = _softplus(small + sp_ref[0:1, :])
    col_s[0] = sp * a_row
    col_s[1] = sp
    col_s[2] = pltpu.roll(_sigmoid(small), LANES - LANE_BETA, axis=1)

    z_s[:, 0:GDN_V] = jnp.dot(h_s[...], wza_ref[...], preferred_element_type=F32)
    z_s[:, GDN_V:MIX_WIDTH] = jnp.dot(h_s[...], wzs_ref[...], preferred_element_type=F32)

    r_t = _row_time(lax.broadcasted_iota(jnp.int32, (CHUNK, CHUNK), 0))
    c_t = _row_time(lax.broadcasted_iota(jnp.int32, (CHUNK, CHUNK), 1))
    tril = r_t >= c_t
    lane_cl = lax.broadcasted_iota(jnp.int32, (CHUNK, LANES), 1)
    r_t2 = _row_time(lax.broadcasted_iota(jnp.int32, (CHUNK, LANES), 0))
    c_t2 = _row_time(lane_cl & (CHUNK - 1))
    tril2 = r_t2 >= c_t2
    strict2 = r_t2 > c_t2
    eye2 = (r_t2 == c_t2).astype(F32)
    first2 = lane_cl < CHUNK
    row_ll = lax.broadcasted_iota(jnp.int32, (LANES, LANES), 0)
    lane_ll = lax.broadcasted_iota(jnp.int32, (LANES, LANES), 1)
    chunk_shift = CHUNK.bit_length() - 1
    diag_blocks = (row_ll >> chunk_shift) == (lane_ll >> chunk_shift)
    zeros_k = jnp.zeros((CHUNK, GDN_DK), BF16)
    zeros_2k = jnp.zeros((CHUNK, 2 * GDN_DK), BF16)

    def pair_mm(x, y):
        yb = _bf(y)
        rhs_bd = jnp.where(diag_blocks, jnp.concatenate([yb, yb], axis=0), 0.0)
        return jnp.dot(_bf(x), rhs_bd, preferred_element_type=F32)

    def pair_rows(t2, l0, l1):
        return jnp.where(lane1 < CHUNK, t2[l0:l0 + 1, :], t2[l1:l1 + 1, :])

    def pair_cols(c, l0, l1):
        return jnp.where(first2, c[:, l0:l0 + 1], c[:, l1:l1 + 1])

    is_g = lane_cl < GDN_HEADS
    neg_inf = -jnp.inf
    n_pairs = SSD_HEADS // 2
    pairs_per_group = n_pairs // SSD_GROUPS
    hs = range(GDN_HEADS)

    def group_body(gi, carry):
        base = gi * (CHUNK * GROUP)
        rows = [pl.ds(pl.multiple_of(base + cc * CHUNK, CHUNK), CHUNK) for cc in range(GROUP)]

        cs, ecs, gl, beta0, cs_t, bd_t, edl_t, edc = [], [], [], [], [], [], [], []
        for cc in range(GROUP):
            gad = col_s[0, rows[cc], :]
            dtv = col_s[1, rows[cc], :]
            b0 = col_s[2, rows[cc], :]
            c_ = _chunk_cumsum(gad)
            last = c_[CHUNK - 1:CHUNK, :]
            e_ = jnp.exp(c_)
            edl = jnp.exp(last - c_)
            cs.append(c_)
            ecs.append(e_)
            gl.append(jnp.exp(last))
            beta0.append(b0)
            cs_t.append(jnp.concatenate([c_, c_], axis=0).T)
            bd_ = jnp.where(is_g, b0, dtv)
            bd_t.append(jnp.concatenate([bd_, bd_], axis=0).T)
            edl_t.append(edl.T)
            edc.append(edl * dtv)

        sst = [sst_s[p] for p in range(n_pairs)]
        prep_jobs, y_jobs, s_jobs = [], [], []
        for cc in range(GROUP):
            cb, b_t, c_g = [], [], []
            ys = [None] * SSD_HEADS
            xp = [None] * n_pairs
            rhs_y = [None] * n_pairs

            def prep_job(cc=cc, cb=cb, b_t=b_t, c_g=c_g, xp=xp):
                for g in range(SSD_GROUPS):
                    bm = act_s[rows[cc], COL_B + g * SSD_STATE:COL_B + (g + 1) * SSD_STATE]
                    cm = act_s[rows[cc], COL_C + g * SSD_STATE:COL_C + (g + 1) * SSD_STATE]
                    cb.append(_dot_nt(cm, bm))
                    b_t.append(_bf(bm.T))
                    c_g.append(cm)
                for p in range(n_pairs):
                    xp[p] = act_s[rows[cc], COL_XS + p * LANES:COL_XS + (p + 1) * LANES]

            def y_job(i, cc=cc, cb=cb, c_g=c_g, xp=xp, ys=ys, rhs_y=rhs_y):
                p, g, ln = i // 2, i // 2 // pairs_per_group, LANE_DT + i
                if i % 2 == 0:
                    rhs_y[p] = jnp.concatenate([_bf(sst[p]), _bf(xp[p])], axis=0)
                lmat = jnp.exp(jnp.where(tril, cs[cc][:, ln:ln + 1] - cs_t[cc][ln:ln + 1, 0:CHUNK], neg_inf))
                mh = cb[g] * lmat * bd_t[cc][ln:ln + 1, 0:CHUNK]
                ce = c_g[g] * ecs[cc][:, ln:ln + 1]
                ys[i] = jnp.dot(jnp.concatenate([_bf(ce), _bf(mh)], axis=1), rhs_y[p], preferred_element_type=F32)

            def s_job(p, cc=cc, b_t=b_t, xp=xp, ys=ys):
                g, l0, l1 = p // pairs_per_group, LANE_DT + 2 * p, LANE_DT + 2 * p + 1
                oy_s[rows[cc], GDN_V + p * LANES:GDN_V + (p + 1) * LANES] = jnp.where(
                    lane_cl < SSD_HEADDIM, ys[2 * p], ys[2 * p + 1])
                coef = jnp.where(lane_cl < SSD_HEADDIM, edc[cc][:, l0:l0 + 1], edc[cc][:, l1:l1 + 1])
                glp = jnp.where(lane1 < SSD_HEADDIM, gl[cc][:, l0:l0 + 1], gl[cc][:, l1:l1 + 1])
                sst[p] = sst[p] * glp + _dot(b_t[g], xp[p] * coef)

            prep_jobs.append(prep_job)
            y_jobs.append([functools.partial(y_job, i) for i in range(SSD_HEADS)])
            s_jobs.append([functools.partial(s_job, p) for p in range(n_pairs)])

        def fill():
            if prep_jobs:
                prep_jobs.pop(0)()

        hp_n = GDN_HEADS // 2
        ch = [(cc, hp) for cc in range(GROUP) for hp in range(hp_n)]
        nch = range(len(ch))

        def head_cols(ref_base, cc, h):
            return act_s[rows[cc], ref_base + h * GDN_DK:ref_base + (h + 1) * GDN_DK]

        q = [[head_cols(0, cc, 2 * hp + e) for e in range(2)] for cc, hp in ch]
        k = [[head_cols(GDN_QK, cc, 2 * hp + e) for e in range(2)] for cc, hp in ch]
        v = [[head_cols(2 * GDN_QK, cc, 2 * hp + e) for e in range(2)] for cc, hp in ch]
        kb = [[_bf(k[i][e]) for e in range(2)] for i in nch]
        kq = []
        for i in nch:
            lhs_kq = jnp.concatenate([jnp.concatenate(kb[i], axis=1),
                                      jnp.concatenate([_bf(q[i][0]), _bf(q[i][1])], axis=1)], axis=0)
            rhs_kq = jnp.concatenate([jnp.concatenate([kb[i][0], zeros_k], axis=1),
                                      jnp.concatenate([zeros_k, kb[i][1]], axis=1)], axis=0)
            kq.append(_dot_nt(lhs_kq, rhs_kq))
        fill()
        dec = [jnp.exp(jnp.where(tril2, pair_cols(cs[cc], 2 * hp, 2 * hp + 1)
                                 - pair_rows(cs_t[cc], 2 * hp, 2 * hp + 1), neg_inf)) for cc, hp in ch]
        a = [jnp.where(strict2, kq[i][0:CHUNK] * pair_cols(beta0[cc], 2 * hp, 2 * hp + 1) * dec[i], 0.0)
             for i, (cc, hp) in enumerate(ch)]
        tinv = _unit_lower_inverse(a, eye2, r_t2, c_t2, pair_mm, fill)
        sol = []
        for i, (cc, hp) in enumerate(ch):
            rhs_h = [jnp.concatenate([_bf(v[i][e]), _bf(k[i][e] * ecs[cc][:, 2 * hp + e:2 * hp + e + 1])], axis=1)
                     for e in range(2)]
            rhs_bd = jnp.concatenate([jnp.concatenate([rhs_h[0], zeros_2k], axis=1),
                                      jnp.concatenate([zeros_2k, rhs_h[1]], axis=1)], axis=0)
            sol.append(_dot(tinv[i] * pair_rows(bd_t[cc], 2 * hp, 2 * hp + 1), rhs_bd))
        fill()
        qkd = [_bf(kq[i][CHUNK:] * dec[i]) for i in nch]

        gst = [gst_s[h] for h in hs]
        sol_w = 2 * GDN_DV
        for cc in range(GROUP):
            wq, u = [], []
            for h in hs:
                i, e = cc * hp_n + h // 2, h % 2
                u.append(sol[i][:, e * sol_w:e * sol_w + GDN_DV])
                wq.append(jnp.concatenate([_bf(sol[i][:, e * sol_w + GDN_DV:(e + 1) * sol_w]),
                                           _bf(q[i][e] * ecs[cc][:, h:h + 1])], axis=0))
            ws = [_dot(wq[h], gst[h]) for h in hs]
            for job in y_jobs[cc]:
                job()
            v_new = [_bf(u[h] - ws[h][0:CHUNK]) for h in hs]
            oqk = []
            for hp in range(hp_n):
                vn_bd = jnp.concatenate([jnp.concatenate([v_new[2 * hp], zeros_k], axis=1),
                                         jnp.concatenate([zeros_k, v_new[2 * hp + 1]], axis=1)], axis=0)
                oqk.append(jnp.dot(qkd[cc * hp_n + hp], vn_bd, preferred_element_type=F32))
            upd = [_dot(k[cc * hp_n + h // 2][h % 2].T * edl_t[cc][h:h + 1, :], v_new[h]) for h in hs]
            for job in s_jobs[cc]:
                job()
            for h in hs:
                oy_s[rows[cc], h * GDN_DV:(h + 1) * GDN_DV] = (
                    ws[h][CHUNK:] + oqk[h // 2][:, (h % 2) * GDN_DV:(h % 2 + 1) * GDN_DV])
                gst[h] = gst[h] * gl[cc][:, h:h + 1] + upd[h]
        for h in hs:
            gst_s[h] = gst[h]
        for p in range(n_pairs):
            sst_s[p] = sst[p]
        return carry

    lax.fori_loop(0, n_groups, group_body, 0)

    for hd in range(GDN_HEADS):
        cols = slice(hd * GDN_DV, (hd + 1) * GDN_DV)
        o = oy_s[:, cols]
        o = o * lax.rsqrt(jnp.mean(o * o, axis=-1, keepdims=True) + EPS) * gnw_ref[:, cols]
        cat_s[:, cols] = _bf(o * _silu(z_s[:, cols]))
    gw = SSD_D // SSD_GROUPS
    for g in range(SSD_GROUPS):
        cols = slice(GDN_V + g * gw, GDN_V + (g + 1) * gw)
        xs = act_s[:, COL_XS + g * gw:COL_XS + (g + 1) * gw]
        y = oy_s[:, cols] + dsk_ref[:, g * gw:(g + 1) * gw] * xs
        y = y * _silu(z_s[:, cols])
        y = y * lax.rsqrt(jnp.mean(y * y, axis=-1, keepdims=True) + EPS) * snw_ref[:, g * gw:(g + 1) * gw]
        cat_s[:, cols] = _bf(y)

    mix = jnp.dot(cat_s[...], wout_ref[...], preferred_element_type=F32)
    _store_natural(o_ref, x_ref, _rmsnorm(mix, pnw_ref[...]), slab_s, 0)


def _ffn_kernel(x_ref, perm_ref, nw_ref, wup_ref, cw_ref, cb_ref, wdn_ref, pnw_ref, o_ref, h_s, carry_s, act_s,
                slab_s):
    tile = x_ref.shape[1]
    sub_rows = tile // FFN_SUB

    @pl.when(pl.program_id(1) == 0)
    def _():
        carry_s[...] = jnp.zeros(carry_s.shape, F32)

    def prenorm(sub):
        rows = slice(sub * sub_rows, (sub + 1) * sub_rows)
        h_s[rows, :] = _to_chunk_order(_bf(_rmsnorm(x_ref[0, rows, :], nw_ref[...])), perm_ref)

    prenorm(0)
    for sub in range(FFN_SUB):
        rows = slice(sub * sub_rows, (sub + 1) * sub_rows)
        for c in range(D_FF // FFN_CW):
            streams = []
            for base in (0, D_FF):
                cols = slice(base + c * FFN_CW, base + (c + 1) * FFN_CW)
                u = jnp.dot(h_s[rows, :], wup_ref[:, cols], preferred_element_type=F32)
                taps = [jnp.broadcast_to(cw_ref[k:k + 1, cols], (SUBLANES, FFN_CW)) for k in range(FFN_CONV_K)]
                streams.append((u, taps, jnp.broadcast_to(cb_ref[:, cols], (SUBLANES, FFN_CW)), cols))
            for r, (g, up) in _conv_chunks(streams, carry_s):
                r0 = sub * sub_rows + r * SUBLANES
                act_s[r0:r0 + SUBLANES, c * FFN_CW:(c + 1) * FFN_CW] = _bf(_silu(g) * up)
        if sub + 1 < FFN_SUB:
            prenorm(sub + 1)
        f = jnp.dot(act_s[rows, :], wdn_ref[...], preferred_element_type=F32)
        _store_natural(o_ref, x_ref, _rmsnorm(f, pnw_ref[...]), slab_s, sub * sub_rows)


def _const_spec(shape):
    return pl.BlockSpec(shape, lambda b, j: (0,) * len(shape), pipeline_mode=pl.Buffered(1))


def _mixer_call(x, nw, wqkv, wxbc, wza, wzs, wsm, cw, cb, sp, gnw, dsk, snw, wout, pnw):
    bsz, seq, _ = x.shape
    tile = MIX_TILE
    assert seq % tile == 0 and tile % (CHUNK * GROUP) == 0 and tile % PERM_BLOCK == 0
    tok_spec = pl.BlockSpec((1, tile, D_MODEL), lambda b, j: (b, j, 0))
    consts = (_perm_matrix(PERM_BLOCK), nw, wqkv, wxbc, wza, wzs, wsm, cw, cb, sp, gnw, dsk, snw, wout, pnw)
    return pl.pallas_call(
        _mixer_kernel,
        grid=(bsz, seq // tile),
        in_specs=[tok_spec] + [_const_spec(c.shape) for c in consts],
        out_specs=tok_spec,
        out_shape=jax.ShapeDtypeStruct(x.shape, x.dtype),
        scratch_shapes=[
            pltpu.VMEM((tile, D_MODEL), BF16),
            pltpu.VMEM(((CONV_K - 1) * SUBLANES, CONV_CH), F32),
            pltpu.VMEM((tile, CONV_CH), F32),
            pltpu.VMEM((3, tile, LANES), F32),
            pltpu.VMEM((tile, MIX_WIDTH), F32),
            pltpu.VMEM((tile, MIX_WIDTH), F32),
            pltpu.VMEM((tile, MIX_WIDTH), BF16),
            pltpu.VMEM((D_MODEL // LANES, tile, LANES), F32),
            pltpu.VMEM((GDN_HEADS, GDN_DK, GDN_DV), F32),
            pltpu.VMEM((SSD_HEADS // 2, SSD_STATE, LANES), F32),
        ],
        compiler_params=pltpu.CompilerParams(
            dimension_semantics=("arbitrary", "arbitrary"), vmem_limit_bytes=VMEM_LIMIT_BYTES),
        name="mixer",
    )(x, *consts)


def _ffn_call(x, nw, wup, cw, cb, wdn, pnw):
    bsz, seq, _ = x.shape
    tile = FFN_TILE
    assert seq % tile == 0 and tile % (FFN_SUB * PERM_BLOCK) == 0
    tok_spec = pl.BlockSpec((1, tile, D_MODEL), lambda b, j: (b, j, 0))
    consts = (_perm_matrix(PERM_BLOCK), nw, wup, cw, cb, wdn, pnw)
    return pl.pallas_call(
        _ffn_kernel,
        grid=(bsz, seq // tile),
        in_specs=[tok_spec] + [_const_spec(c.shape) for c in consts],
        out_specs=tok_spec,
        out_shape=jax.ShapeDtypeStruct(x.shape, x.dtype),
        scratch_shapes=[
            pltpu.VMEM((tile, D_MODEL), BF16),
            pltpu.VMEM(((FFN_CONV_K - 1) * SUBLANES, 2 * D_FF), F32),
            pltpu.VMEM((tile, D_FF), BF16),
            pltpu.VMEM((D_MODEL // LANES, tile, LANES), F32),
        ],
        compiler_params=pltpu.CompilerParams(
            dimension_semantics=("arbitrary", "arbitrary"), vmem_limit_bytes=VMEM_LIMIT_BYTES),
        name="ffn",
    )(x, *consts)


def _row(v):
    return v.reshape(1, -1).astype(F32)


def _layer(x, pre_mix_norm, w_in, gdn_conv_w, gdn_a_log, gdn_dt_bias, gdn_norm_w, ssd_conv_w, ssd_conv_b,
           ssd_a_log, ssd_dt_bias, ssd_d, ssd_norm_w, w_out, post_mix_norm, pre_ffn_norm, w_up, ffn_conv_w,
           ffn_conv_b, w_down, post_ffn_norm):
    offs = [0]
    for n in IN_SPLITS:
        offs.append(offs[-1] + n)
    wqkv = _bf(w_in[:, offs[0]:offs[3]])
    wza = _bf(w_in[:, offs[3]:offs[4]])
    wzs = _bf(w_in[:, offs[6]:offs[7]])
    wxbc = _bf(w_in[:, offs[7]:offs[10]])
    pad = jnp.zeros((D_MODEL, LANES - N_GATE_LANES - GDN_HEADS), w_in.dtype)
    wsm = _bf(jnp.concatenate([w_in[:, offs[5]:offs[6]], w_in[:, offs[10]:offs[11]], w_in[:, offs[4]:offs[5]], pad],
                              axis=1))
    cw = jnp.concatenate([gdn_conv_w, ssd_conv_w], axis=1).astype(F32)
    cb = jnp.concatenate([jnp.zeros((GDN_CONV_CH,), F32), ssd_conv_b.astype(F32)]).reshape(1, CONV_CH)
    zpad = jnp.zeros((LANES - N_GATE_LANES,), F32)
    sp = jnp.zeros((SUBLANES, LANES), F32)
    sp = sp.at[0].set(jnp.concatenate([gdn_dt_bias.astype(F32), ssd_dt_bias.astype(F32), zpad]))
    sp = sp.at[1].set(jnp.concatenate([gdn_a_log.astype(F32), ssd_a_log.astype(F32), zpad]))
    gnw = _row(jnp.tile(gdn_norm_w, GDN_HEADS))
    dsk = _row(jnp.repeat(ssd_d, SSD_HEADDIM))
    x1 = _mixer_call(x, _row(pre_mix_norm), wqkv, wxbc, wza, wzs, wsm, cw, cb, sp, gnw, dsk, _row(ssd_norm_w),
                     _bf(w_out), _row(post_mix_norm))
    return _ffn_call(x1, _row(pre_ffn_norm), _bf(w_up), ffn_conv_w.astype(F32), _row(ffn_conv_b), _bf(w_down),
                     _row(post_ffn_norm))


def kernel(x, pre_mix_norm, w_in, gdn_conv_w, gdn_a_log, gdn_dt_bias, gdn_norm_w, ssd_conv_w, ssd_conv_b, ssd_a_log, ssd_dt_bias, ssd_d, ssd_norm_w, w_out, post_mix_norm, pre_ffn_norm, w_up, ffn_conv_w, ffn_conv_b, w_down, post_ffn_norm):
    params = (pre_mix_norm, w_in, gdn_conv_w, gdn_a_log, gdn_dt_bias, gdn_norm_w, ssd_conv_w, ssd_conv_b, ssd_a_log,
              ssd_dt_bias, ssd_d, ssd_norm_w, w_out, post_mix_norm, pre_ffn_norm, w_up, ffn_conv_w, ffn_conv_b,
              w_down, post_ffn_norm)
    for layer in range(pre_mix_norm.shape[0]):
        x = _layer(x, *(p[layer] for p in params))
    return x
```
